```python
import jax
import jax.numpy as jnp
from jax import lax
import numpy as np

D_MODEL = 1024
BATCH = 32
SEQ = 2048
DEPTH = 1

HEAD_DIM = 64
NSA_HEADS = 8
NSA_GROUPS = 2
NSA_CMP_LEN = 32
NSA_CMP_STRIDE = 16
NSA_CMP_HIDDEN = 128
NSA_SLC_LEN = 64
NSA_SLC_TOPN = 16
NSA_WINDOW = 512
DSA_HEADS = 8
IDX_HEADS = 8
IDX_DIM = 64
DSA_TOPK_MAX = 256
D_FF = ((8 * D_MODEL + 3 * 256 - 1) // (3 * 256)) * 256
N_ADA = 6
Q_BLOCK = 128
SLC_Q_BLOCK = 16
RMS_EPS = 1e-6
FORCE_SCORE = 1e9

IN_SIZES = (
    NSA_HEADS * HEAD_DIM,
    NSA_GROUPS * HEAD_DIM,
    NSA_GROUPS * HEAD_DIM,
    NSA_GROUPS * HEAD_DIM,
    NSA_GROUPS * HEAD_DIM,
    NSA_GROUPS * HEAD_DIM,
    NSA_GROUPS * HEAD_DIM,
    NSA_HEADS * 3,
    DSA_HEADS * HEAD_DIM,
    HEAD_DIM,
    HEAD_DIM,
    IDX_HEADS * IDX_DIM,
    IDX_DIM,
    IDX_HEADS,
    D_MODEL,
    D_MODEL,
)
D_IN = sum(IN_SIZES)

kernel_name = 'hybrid_nsa_dsa_gated_block'


def rms_norm(x, g):
    xf = x.astype(jnp.float32)
    y = xf * lax.rsqrt(jnp.mean(xf * xf, axis=-1, keepdims=True) + RMS_EPS)
    return (y * g.astype(jnp.float32)).astype(x.dtype)


def masked_softmax(s, mask):
    s = jnp.where(mask, s.astype(jnp.float32), -jnp.inf)
    m = jnp.max(s, axis=-1, keepdims=True)
    m = jnp.where(jnp.isfinite(m), m, 0.0)
    e = jnp.where(mask, jnp.exp(s - m), 0.0)
    return e / jnp.maximum(jnp.sum(e, axis=-1, keepdims=True), 1e-30)


def alibi_slopes(n_heads):
    return jnp.exp2(-8.0 * jnp.arange(1, n_heads + 1, dtype=jnp.float32) / n_heads)


def block_map(fn, n_rows, block):
    out = lax.map(fn, jnp.arange(n_rows // block))
    out = jnp.moveaxis(out, 0, 1)
    return out.reshape((out.shape[0], n_rows) + out.shape[3:])


def nsa_compress(t, pe, w1, w2):
    b, s, g, hd = t.shape
    n_cmp = (s - NSA_CMP_LEN) // NSA_CMP_STRIDE + 1
    idx = (jnp.arange(n_cmp) * NSA_CMP_STRIDE)[:, None] + jnp.arange(NSA_CMP_LEN)[None, :]
    blk = t[:, idx] + pe[:, None, :].astype(t.dtype)
    blk = jnp.moveaxis(blk, 3, 2).reshape(b, n_cmp, g, NSA_CMP_LEN * hd)
    return jax.nn.silu(blk @ w1) @ w2


def nsa_mixer(q, k_cmp, v_cmp, k_slc, v_slc, k_win, v_win, gates,
              g_q, g_kc, g_ks, g_kw, pe_ck, pe_cv, w_ck1, w_ck2, w_cv1, w_cv2):
    b, s, h, hd = q.shape
    g = k_cmp.shape[2]
    r = h // g
    f32 = jnp.float32
    slopes = alibi_slopes(h).reshape(g, r, 1, 1)
    qg = (rms_norm(q, g_q) * hd ** -0.5).reshape(b, s, g, r, hd)
    pos = jnp.arange(s)

    kc = rms_norm(nsa_compress(k_cmp, pe_ck, w_ck1, w_ck2), g_kc)
    vc = nsa_compress(v_cmp, pe_cv, w_cv1, w_cv2)
    n_cmp = kc.shape[1]
    cmp_start = jnp.arange(n_cmp) * NSA_CMP_STRIDE
    dist_c = (pos[:, None] - (cmp_start + NSA_CMP_LEN - 1)[None, :]).astype(f32)
    s_c = jnp.einsum('bsgrd,bngd->bgrsn', qg, kc).astype(f32) - slopes * dist_c
    p_c = masked_softmax(s_c, dist_c >= 0)
    o_cmp = jnp.einsum('bgrsn,bngd->bsgrd', p_c.astype(vc.dtype), vc)

    n_slc = s // NSA_SLC_LEN
    j = jnp.arange(n_slc)
    overlap = ((cmp_start[:, None] < (j[None, :] + 1) * NSA_SLC_LEN) &
               (cmp_start[:, None] + NSA_CMP_LEN > j[None, :] * NSA_SLC_LEN)).astype(f32)
    imp = jnp.einsum('bgrsn,nj->bgsj', p_c, overlap)
    cur = (pos // NSA_SLC_LEN)[:, None]
    forced = (j[None, :] == 0) | (j[None, :] == cur) | (j[None, :] == cur - 1)
    visible = j[None, :] * NSA_SLC_LEN <= pos[:, None]
    imp = jnp.where(forced, FORCE_SCORE, jnp.where(visible, imp, -FORCE_SCORE))
    n_top = min(NSA_SLC_TOPN, n_slc)
    _, sel = lax.top_k(imp, n_top)

    ks_blk = rms_norm(k_slc, g_ks).reshape(b, n_slc, NSA_SLC_LEN, g, hd).transpose(0, 3, 1, 2, 4)
    vs_blk = v_slc.reshape(b, n_slc, NSA_SLC_LEN, g, hd).transpose(0, 3, 1, 2, 4)
    take_blocks = jax.vmap(jax.vmap(lambda blocks, ix: blocks[ix]))
    n_keys = n_top * NSA_SLC_LEN

    def slc_block(i):
        q0 = i * SLC_Q_BLOCK
        qc = lax.dynamic_slice_in_dim(qg, q0, SLC_Q_BLOCK, axis=1)
        ic = lax.dynamic_slice_in_dim(sel, q0, SLC_Q_BLOCK, axis=2)
        kg = take_blocks(ks_blk, ic).reshape(b, g, SLC_Q_BLOCK, n_keys, hd)
        vg = take_blocks(vs_blk, ic).reshape(b, g, SLC_Q_BLOCK, n_keys, hd)
        kpos = (ic[..., None] * NSA_SLC_LEN + jnp.arange(NSA_SLC_LEN)).reshape(b, g, SLC_Q_BLOCK, n_keys)
        tq = q0 + jnp.arange(SLC_Q_BLOCK)
        dist = (tq[None, None, :, None] - kpos).astype(f32)[:, :, None]
        sc = jnp.einsum('bqgrd,bgqkd->bgrqk', qc, kg).astype(f32) - slopes * dist
        p = masked_softmax(sc, dist >= 0)
        return jnp.einsum('bgrqk,bgqkd->bqgrd', p.astype(vg.dtype), vg)

    o_slc = block_map(slc_block, s, SLC_Q_BLOCK)

    pad = ((0, 0), (NSA_WINDOW, 0), (0, 0), (0, 0))
    kw_p = jnp.pad(rms_norm(k_win, g_kw), pad)
    vw_p = jnp.pad(v_win, pad)
    span = NSA_WINDOW + Q_BLOCK

    def win_block(i):
        q0 = i * Q_BLOCK
        qc = lax.dynamic_slice_in_dim(qg, q0, Q_BLOCK, axis=1)
        kb = lax.dynamic_slice_in_dim(kw_p, q0, span, axis=1)
        vb = lax.dynamic_slice_in_dim(vw_p, q0, span, axis=1)
        tq = q0 + jnp.arange(Q_BLOCK)
        kpos = q0 - NSA_WINDOW + jnp.arange(span)
        dist_i = tq[:, None] - kpos[None, :]
        mask = (dist_i >= 0) & (dist_i < NSA_WINDOW) & (kpos[None, :] >= 0)
        sc = jnp.einsum('bqgrd,bkgd->bgrqk', qc, kb).astype(f32) - slopes * dist_i.astype(f32)
        p = masked_softmax(sc, mask)
        return jnp.einsum('bgrqk,bkgd->bqgrd', p.astype(vb.dtype), vb)

    o_win = block_map(win_block, s, Q_BLOCK)

    gt = jax.nn.sigmoid(gates.astype(f32)).astype(q.dtype).reshape(b, s, g, r, 3)
    o = gt[..., 0:1] * o_cmp + gt[..., 1:2] * o_slc + gt[..., 2:3] * o_win
    return o.reshape(b, s, h * hd)


def dsa_mixer(q, k, v, iq, ik, iw, g_q, g_k):
    b, s, h, hd = q.shape
    f32 = jnp.float32
    topk = min(DSA_TOPK_MAX, s // 4)
    slopes = alibi_slopes(h)[:, None, None]
    qn = rms_norm(q, g_q) * hd ** -0.5
    kn = rms_norm(k, g_k)
    iq = iq * IDX_DIM ** -0.5
    iw = iw * IDX_HEADS ** -0.5
    pos = jnp.arange(s)
    take_keys = jax.vmap(lambda t, ix: t[ix])

    def dsa_block(i):
        q0 = i * Q_BLOCK
        tq = q0 + jnp.arange(Q_BLOCK)
        iqc = lax.dynamic_slice_in_dim(iq, q0, Q_BLOCK, axis=1)
        iwc = lax.dynamic_slice_in_dim(iw, q0, Q_BLOCK, axis=1)
        logits = jax.nn.relu(jnp.einsum('bqhd,bsd->bqhs', iqc, ik))
        score = jnp.einsum('bqh,bqhs->bqs', iwc, logits).astype(f32)
        score = jnp.where(pos[None, None, :] <= tq[None, :, None], score, -jnp.inf)
        _, sel = lax.top_k(score, topk)
        kg = take_keys(kn, sel)
        vg = take_keys(v, sel)
        dist = (tq[None, :, None] - sel).astype(f32)[:, None]
        qc = lax.dynamic_slice_in_dim(qn, q0, Q_BLOCK, axis=1)
        sc = jnp.einsum('bqhd,bqkd->bhqk', qc, kg).astype(f32) - slopes * dist
        p = masked_softmax(sc, dist >= 0)
        return jnp.einsum('bhqk,bqkd->bqhd', p.astype(vg.dtype), vg)

    o = block_map(dsa_block, s, Q_BLOCK)
    return o.reshape(b, s, h * hd)


def hybrid_layer(x, c, w_ada, b_ada, g_norm1, g_norm2, w_in, g_q_a, g_kc_a, g_ks_a, g_kw_a,
                 pe_ck, pe_cv, w_ck1, w_ck2, w_cv1, w_cv2, g_q_b, g_k_b, w_o_a, w_o_b, w_out,
                 w_ff_gate, w_ff_up, w_ff_down):
    b, s, _ = x.shape
    mod = (jax.nn.silu(c) @ w_ada + b_ada)[:, None, :]
    shift1, scale1, gate1, shift2, scale2, gate2 = jnp.split(mod, N_ADA, axis=-1)

    h = rms_norm(x, g_norm1) * (1.0 + scale1) + shift1
    proj = h @ w_in
    (q_a, kc_a, vc_a, ks_a, vs_a, kw_a, vw_a, gate_nsa,
     q_b, k_b, v_b, iq_b, ik_b, iw_b, gate_a, gate_b) = jnp.split(
        proj, np.cumsum(IN_SIZES)[:-1].tolist(), axis=-1)

    def heads(t, n):
        return t.reshape(b, s, n, HEAD_DIM)

    o_a = nsa_mixer(heads(q_a, NSA_HEADS), heads(kc_a, NSA_GROUPS), heads(vc_a, NSA_GROUPS),
                    heads(ks_a, NSA_GROUPS), heads(vs_a, NSA_GROUPS),
                    heads(kw_a, NSA_GROUPS), heads(vw_a, NSA_GROUPS),
                    gate_nsa.reshape(b, s, NSA_HEADS, 3),
                    g_q_a, g_kc_a, g_ks_a, g_kw_a, pe_ck, pe_cv, w_ck1, w_ck2, w_cv1, w_cv2)
    o_b = dsa_mixer(heads(q_b, DSA_HEADS), k_b, v_b, iq_b.reshape(b, s, IDX_HEADS, IDX_DIM),
                    ik_b, iw_b, g_q_b, g_k_b)
    y = jax.nn.sigmoid(gate_a) * (o_a @ w_o_a) + jax.nn.sigmoid(gate_b) * (o_b @ w_o_b)
    x = x + gate1 * (y @ w_out)

    h2 = rms_norm(x, g_norm2) * (1.0 + scale2) + shift2
    ff = (jax.nn.silu(h2 @ w_ff_gate) * (h2 @ w_ff_up)) @ w_ff_down
    return x + gate2 * ff


def setup_inputs(seed: int = 0) -> dict:
    key = jax.random.key(seed)
    ks = jax.random.split(key, 32)
    f32 = jnp.float32
    hd = HEAD_DIM
    L = DEPTH

    def nrm(k, shape, scale):
        return jax.random.normal(k, shape, f32) * scale

    def gain(k, n):
        return 1.0 + 0.1 * jax.random.normal(k, (L, n), f32)

    cmp_in = NSA_CMP_LEN * hd
    return {
        'x': nrm(ks[0], (BATCH, SEQ, D_MODEL), 1.0),
        'c': nrm(ks[1], (BATCH, D_MODEL), 1.0),
        'w_ada': nrm(ks[2], (L, D_MODEL, N_ADA * D_MODEL), 0.5 * D_MODEL ** -0.5),
        'b_ada': nrm(ks[3], (L, N_ADA * D_MODEL), 0.01),
        'g_norm1': gain(ks[4], D_MODEL),
        'g_norm2': gain(ks[5], D_MODEL),
        'w_in': nrm(ks[6], (L, D_MODEL, D_IN), D_MODEL ** -0.5),
        'g_q_a': gain(ks[7], hd),
        'g_kc_a': gain(ks[8], hd),
        'g_ks_a': gain(ks[9], hd),
        'g_kw_a': gain(ks[10], hd),
        'pe_ck': nrm(ks[11], (L, NSA_CMP_LEN, hd), 0.1),
        'pe_cv': nrm(ks[12], (L, NSA_CMP_LEN, hd), 0.1),
        'w_ck1': nrm(ks[13], (L, cmp_in, NSA_CMP_HIDDEN), cmp_in ** -0.5),
        'w_ck2': nrm(ks[14], (L, NSA_CMP_HIDDEN, hd), NSA_CMP_HIDDEN ** -0.5),
        'w_cv1': nrm(ks[15], (L, cmp_in, NSA_CMP_HIDDEN), cmp_in ** -0.5),
        'w_cv2': nrm(ks[16], (L, NSA_CMP_HIDDEN, hd), NSA_CMP_HIDDEN ** -0.5),
        'g_q_b': gain(ks[17], hd),
        'g_k_b': gain(ks[18], hd),
        'w_o_a': nrm(ks[19], (L, NSA_HEADS * hd, D_MODEL), (NSA_HEADS * hd) ** -0.5),
        'w_o_b': nrm(ks[20], (L, DSA_HEADS * hd, D_MODEL), (DSA_HEADS * hd) ** -0.5),
        'w_out': nrm(ks[21], (L, D_MODEL, D_MODEL), D_MODEL ** -0.5),
        'w_ff_gate': nrm(ks[22], (L, D_MODEL, D_FF), D_MODEL ** -0.5),
        'w_ff_up': nrm(ks[23], (L, D_MODEL, D_FF), D_MODEL ** -0.5),
        'w_ff_down': nrm(ks[24], (L, D_FF, D_MODEL), D_FF ** -0.5),
    }


def reference(x, c, w_ada, b_ada, g_norm1, g_norm2, w_in, g_q_a, g_kc_a, g_ks_a, g_kw_a,
              pe_ck, pe_cv, w_ck1, w_ck2, w_cv1, w_cv2, g_q_b, g_k_b, w_o_a, w_o_b, w_out,
              w_ff_gate, w_ff_up, w_ff_down):
    for l in range(DEPTH):
        x = hybrid_layer(x, c, w_ada[l], b_ada[l], g_norm1[l], g_norm2[l], w_in[l],
                         g_q_a[l], g_kc_a[l], g_ks_a[l], g_kw_a[l], pe_ck[l], pe_cv[l],
                         w_ck1[l], w_ck2[l], w_cv1[l], w_cv2[l], g_q_b[l], g_k_b[l],
                         w_o_a[l], w_o_b[l], w_out[l], w_ff_gate[l], w_ff_up[l], w_ff_down[l])
    return x
```

```python
import functools

import numpy as np
import jax
import jax.numpy as jnp
from jax import lax
from jax.experimental import pallas as pl
from jax.experimental.pallas import tpu as pltpu

F32 = jnp.float32
BF16 = jnp.bfloat16
I32 = jnp.int32

HEAD_DIM = 64
N_HEADS = 8
NSA_GROUPS = 2
CMP_LEN = 32
CMP_STRIDE = 16
CMP_HIDDEN = 128
SLC_LEN = 64
SLC_TOPN = 16
WINDOW = 512
DSA_TOPK_MAX = 256
N_ADA = 6
RMS_EPS = 1e-6
FORCE_SCORE = 1e9
NEG = -1e30
MASK_BIG = 2.0 ** 100
INT_MIN = -(2 ** 31)

TQ = 256
TK = 256
TM_PROJ = 512
VMEM_LIMIT = 56 * 1024 * 1024


def _dot(a, b):
    return jnp.dot(a, b, preferred_element_type=F32)


def _dot_nt(a, b):
    return lax.dot_general(a, b, (((1,), (1,)), ((), ())), preferred_element_type=F32)


def _sigmoid(x):
    return 1.0 / (1.0 + jnp.exp(-x))


def _params(*sem):
    return pltpu.CompilerParams(dimension_semantics=sem, vmem_limit_bytes=VMEM_LIMIT)


def _const_spec(shape):
    nd = len(shape)
    return pl.BlockSpec(shape, lambda *_: (0,) * nd)


def _mod_kernel(c_ref, w_ref, b_ref, o_ref):
    c = c_ref[...]
    a = c * _sigmoid(c)
    o_ref[...] = jnp.dot(a, w_ref[...], preferred_element_type=F32,
                         precision=lax.Precision.HIGHEST) + b_ref[...]


def _mod_call(c, w_ada, b_ada):
    bsz, d = c.shape
    n = w_ada.shape[1]
    tn = 512
    return pl.pallas_call(
        _mod_kernel,
        out_shape=jax.ShapeDtypeStruct((bsz, n), F32),
        grid=(n // tn,),
        in_specs=[pl.BlockSpec((bsz, d), lambda j: (0, 0)),
                  pl.BlockSpec((d, tn), lambda j: (0, j)),
                  pl.BlockSpec((1, tn), lambda j: (0, j))],
        out_specs=pl.BlockSpec((bsz, tn), lambda j: (0, j)),
        compiler_params=_params("arbitrary"),
        name="mod",
    )(c, w_ada, b_ada.reshape(1, n))


_C_QA, _C_KVC, _C_KS, _C_VS, _C_KW, _C_VW = 0, 512, 768, 896, 1024, 1152
_C_GN, _C_QB, _C_KB, _C_VB, _C_IK, _C_IQ, _C_IW, _C_GATE = 1280, 1408, 1920, 2048, 2176, 2304, 2816, 2944
_C_END = 4992
_G_QA, _G_KS, _G_KW, _G_QB, _G_KB, _G_END = 0, 512, 640, 768, 1280, 1408


def _pos_features(pos, lane, base):
    a = (pos >> 6).astype(F32)
    b = (pos & 63).astype(F32)
    return jnp.where(lane == base, a, jnp.where(lane == base + 1, b, 0.0))


def _inproj_kernel(x_ref, mod_ref, g1_ref, w_ref, bd_ref, gain_ref,
                   qa_ref, kvc_ref, nkv_ref, gn_ref, qb_ref, dkv_ref, iq_ref, iw_ref, gate_ref):
    tm = x_ref.shape[1]
    x = x_ref[0]
    m = mod_ref[0]
    shift, scale = m[0:1], m[1:2]
    xn = x * lax.rsqrt(jnp.mean(x * x, axis=-1, keepdims=True) + RMS_EPS) * g1_ref[...]
    h = (xn * (1.0 + scale) + shift).astype(BF16)

    def seg(a, b):
        return _dot(h, w_ref[:, a:b])

    def normed(acc, goff):
        w = acc.shape[1]
        cw = min(w, 256)
        outs = []
        for c in range(0, w, cw):
            a = acc[:, c:c + cw]
            ms = _dot((a * a).astype(BF16), bd_ref[:cw, :cw])
            outs.append(a * lax.rsqrt(ms + RMS_EPS) * gain_ref[:, goff + c:goff + c + cw])
        return outs[0] if len(outs) == 1 else jnp.concatenate(outs, axis=1)

    row = lax.broadcasted_iota(I32, (tm, 128), 0)
    lane = lax.broadcasted_iota(I32, (tm, 128), 1)
    lo = lane < 64
    pos = pl.program_id(1) * tm + row
    feat_hi = _pos_features(pos, lane, 64)
    feat_lo = _pos_features(pos, lane, 0)

    qa_ref[0] = normed(seg(_C_QA, _C_QA + 512), _G_QA).astype(BF16)
    kvc_ref[0] = seg(_C_KVC, _C_KVC + 256)

    ksn = normed(seg(_C_KS, _C_KS + 128), _G_KS)
    nkv_ref[0, :, 0:128] = jnp.where(lo, ksn, feat_hi).astype(BF16)
    nkv_ref[0, :, 128:256] = jnp.where(lo, feat_lo, ksn).astype(BF16)
    nkv_ref[0, :, 256:384] = seg(_C_VS, _C_VS + 128).astype(BF16)
    kwn = normed(seg(_C_KW, _C_KW + 128), _G_KW)
    nkv_ref[0, :, 384:512] = jnp.where(lo, kwn, feat_hi).astype(BF16)
    nkv_ref[0, :, 512:640] = jnp.where(lo, feat_lo, kwn).astype(BF16)
    nkv_ref[0, :, 640:768] = seg(_C_VW, _C_VW + 128).astype(BF16)

    gn_ref[0] = seg(_C_GN, _C_GN + 128)

    qb_ref[0] = normed(seg(_C_QB, _C_QB + 512), _G_QB).astype(BF16)
    kbn = normed(seg(_C_KB, _C_KB + 128), _G_KB)
    dkv_ref[0, :, 0:128] = jnp.where(lo, kbn, feat_hi).astype(BF16)
    dkv_ref[0, :, 128:256] = jnp.where(lo, feat_lo, kbn).astype(BF16)
    dkv_ref[0, :, 256:384] = seg(_C_VB, _C_VB + 128).astype(BF16)
    dkv_ref[0, :, 384:512] = seg(_C_IK, _C_IK + 128).astype(BF16)

    iq_ref[0] = seg(_C_IQ, _C_IQ + 512).astype(BF16)
    iw_ref[0] = seg(_C_IW, _C_IW + 128) * (N_HEADS ** -0.5)
    for c in range(0, 2048, 512):
        gate_ref[0, :, c:c + 512] = seg(_C_GATE + c, _C_GATE + c + 512).astype(BF16)


def _inproj_call(x, mod, g1, w_all, bd, gains):
    bsz, s, d = x.shape
    tm = TM_PROJ
    widths = (512, 256, 768, 128, 512, 512, 512, 128, 2048)
    dtypes = (BF16, F32, BF16, F32, BF16, BF16, BF16, F32, BF16)
    return pl.pallas_call(
        _inproj_kernel,
        out_shape=[jax.ShapeDtypeStruct((bsz, s, w), dt) for w, dt in zip(widths, dtypes)],
        grid=(bsz, s // tm),
        in_specs=[pl.BlockSpec((1, tm, d), lambda b, i: (b, i, 0)),
                  pl.BlockSpec((1, N_ADA, d), lambda b, i: (b, 0, 0)),
                  _const_spec((1, d)),
                  _const_spec(w_all.shape),
                  _const_spec(bd.shape),
                  _const_spec(gains.shape)],
        out_specs=[pl.BlockSpec((1, tm, w), lambda b, i: (b, i, 0)) for w in widths],
        compiler_params=_params("arbitrary", "arbitrary"),
        name="inproj",
    )(x, mod, g1, w_all, bd, gains)


def _compress_kernel(t_ref, wa_ref, wb_ref, pek_ref, pev_ref, w1k_ref, w1v_ref, w2_ref,
                     gkc_ref, bd_ref, o_ref):
    n = t_ref.shape[1]
    t = t_ref[0].astype(BF16)
    first = _dot(t, wa_ref[...])
    second = _dot(t, wb_ref[...])
    second = pltpu.roll(second, n - 1, 0)
    ck = _dot(jnp.broadcast_to(pek_ref[...], (8, pek_ref.shape[1])).astype(BF16), w1k_ref[...])[0:1]
    cv = _dot(jnp.broadcast_to(pev_ref[...], (8, pev_ref.shape[1])).astype(BF16), w1v_ref[...])[0:1]
    hid = first + second + jnp.concatenate([ck, ck, cv, cv], axis=1)
    act = hid * _sigmoid(hid)
    out = _dot(act.astype(BF16), w2_ref[...])
    kc = out[:, 0:128]
    ms = _dot((kc * kc).astype(BF16), bd_ref[...])
    kcn = kc * lax.rsqrt(ms + RMS_EPS) * gkc_ref[...]
    row = lax.broadcasted_iota(I32, (n, 128), 0)
    lane = lax.broadcasted_iota(I32, (n, 128), 1)
    lo = lane < 64
    cpos = row * CMP_STRIDE + (CMP_LEN - 1)
    o_ref[0, :, 0:128] = jnp.where(lo, kcn, _pos_features(cpos, lane, 64)).astype(BF16)
    o_ref[0, :, 128:256] = jnp.where(lo, _pos_features(cpos, lane, 0), kcn).astype(BF16)
    o_ref[0, :, 256:384] = out[:, 128:256].astype(BF16)


def _compress_call(t16, wa, wb, pek, pev, w1k, w1v, w2, gkc, bd):
    bsz, n, k = t16.shape
    args = (wa, wb, pek, pev, w1k, w1v, w2, gkc, bd)
    return pl.pallas_call(
        _compress_kernel,
        out_shape=jax.ShapeDtypeStruct((bsz, n, 384), BF16),
        grid=(bsz,),
        in_specs=[pl.BlockSpec((1, n, k), lambda b: (b, 0, 0))] + [_const_spec(a.shape) for a in args],
        out_specs=pl.BlockSpec((1, n, 384), lambda b: (b, 0, 0)),
        compiler_params=_params("arbitrary"),
        name="compress",
    )(t16, *args)


def _causal_add(tq, tk):
    r = lax.broadcasted_iota(I32, (tq, tk), 0)
    c = lax.broadcasted_iota(I32, (tq, tk), 1)
    return r, c


def _softmax_pv(s_ref, n_tiles, m_ref, v_tile, ls_ref, acc_ref):
    ls_ref[...] = jnp.zeros_like(ls_ref)
    acc_ref[...] = jnp.zeros_like(acc_ref)

    def body(kt, carry):
        s = s_ref[kt]
        m_rep = m_ref[...]
        halves = [jnp.exp(s[:, c:c + 128] - m_rep) for c in range(0, s.shape[1], 128)]
        tot = halves[0]
        for hv in halves[1:]:
            tot = tot + hv
        ls_ref[...] += tot
        p = jnp.concatenate(halves, axis=1).astype(BF16)
        acc_ref[...] += _dot(p, v_tile(kt))
        return carry

    lax.fori_loop(0, n_tiles, body, 0)
    l = jnp.sum(ls_ref[...], axis=-1, keepdims=True)
    return acc_ref[...] / jnp.maximum(l, 1e-30)


def _fold_max(s):
    out = s[:, 0:128]
    for c in range(128, s.shape[1], 128):
        out = jnp.maximum(out, s[:, c:c + 128])
    return out


def _nsa_kernel(qa_ref, ckv_ref, kv_ref, gn_ref, qf_ref, ov_ref, e3_ref, o_ref,
                qop_ref, s_ref, mx_ref, ls_ref, acc_ref, og_ref, selm_ref):
    tq, tk = TQ, TK
    hg = N_HEADS // NSA_GROUPS
    rows = hg * tq
    qi = pl.program_id(1)
    q0 = qi * tq
    row = lax.broadcasted_iota(I32, (tq, 128), 0)
    lane = lax.broadcasted_iota(I32, (tq, 128), 1)
    lo = lane < 64
    pos = q0 + row

    qblk = qa_ref[0]
    qf = qf_ref[...]
    for j in range(hg):
        blk = qblk[:, j * 128:(j + 1) * 128]
        qop_ref[j * tq:(j + 1) * tq, :] = jnp.where(lo, blk, qf[j:j + 1, :])
        qop_ref[(hg + j) * tq:(hg + j + 1) * tq, :] = jnp.where(lo, qf[hg + j:hg + j + 1, :], blk)

    gt = _sigmoid(gn_ref[0])

    def gate_col(h, br):
        return gt[:, 3 * h + br:3 * h + br + 1]

    r2, c2 = _causal_add(tq, tk)
    causal = jnp.where(c2 <= r2, 0.0, NEG)
    older = jnp.where(c2 > r2, 0.0, NEG)

    for g in range(NSA_GROUPS):
        qg = qop_ref[g * rows:(g + 1) * rows, :]

        kc = ckv_ref[0, :, g * 128:(g + 1) * 128]
        vc = ckv_ref[0, :, 256:384]
        seen = (lane * CMP_STRIDE + (CMP_LEN - 1)) <= pos
        sm = (_dot_nt(qg, kc).reshape(hg, tq, 128) + jnp.where(seen, 0.0, NEG)[None]).reshape(rows, 128)
        m = jnp.max(sm, axis=-1, keepdims=True)
        e = (jnp.exp(sm - m).reshape(hg, tq, 128) * jnp.where(seen, 1.0, 0.0)[None]).reshape(rows, 128)
        p = e / jnp.maximum(jnp.sum(e, axis=-1, keepdims=True), 1e-30)
        o = _dot(p.astype(BF16), vc)
        p = p.reshape(hg, tq, 128)
        for r in range(hg):
            h = g * hg + r
            og_ref[h * tq:(h + 1) * tq, :] = gate_col(h, 0) * o[r * tq:(r + 1) * tq]

        psum = p[0]
        for r in range(1, hg):
            psum = psum + p[r]
        p_hi = psum.astype(BF16)
        p_lo = (psum - p_hi.astype(F32)).astype(BF16)
        imp = _dot(p_hi, ov_ref[...]) + _dot(p_lo, ov_ref[...])
        cur = pos >> 6
        forced = (lane == 0) | (lane == cur) | (lane == cur - 1)
        visible = (lane * SLC_LEN) <= pos
        impf = jnp.where(forced, FORCE_SCORE, jnp.where(visible, imp, -FORCE_SCORE))
        rank = jnp.zeros((tq, 128), F32)
        n_slc = e3_ref.shape[0] * tk // SLC_LEN
        for i in range(n_slc):
            col = impf[:, i:i + 1]
            rank = rank + jnp.where(col > impf, 1.0, 0.0) + jnp.where((col == impf) & (lane > i), 1.0, 0.0)
        selm_ref[g] = jnp.where(rank < float(SLC_TOPN), 0.0, -1.0).astype(BF16)

        selm1 = selm_ref[g]
        kcol = g * 128

        def slc_scores(kt, extra):
            k0 = pl.multiple_of(kt * tk, tk)
            kt_ = kv_ref[0, pl.ds(k0, tk), kcol:kcol + 128]
            sc = _dot_nt(qg, kt_).reshape(hg, tq, tk)
            madd = _dot(selm1, e3_ref[kt])
            if extra is not None:
                madd = madd + extra
            return (sc + madd[None]).reshape(rows, tk)

        mx_ref[...] = jnp.full(mx_ref.shape, NEG, F32)

        def slc_a(kt, carry):
            sc = slc_scores(kt, None)
            s_ref[kt] = sc
            mx_ref[...] = jnp.maximum(mx_ref[...], _fold_max(sc))
            return carry

        lax.fori_loop(0, qi, slc_a, 0)
        sc = slc_scores(qi, causal)
        s_ref[qi] = sc
        mx = jnp.maximum(mx_ref[...], _fold_max(sc))
        m_rep = jnp.broadcast_to(jnp.max(mx, axis=-1, keepdims=True), (rows, 128))
        mx_ref[...] = m_rep

        def slc_v(kt):
            k0 = pl.multiple_of(kt * tk, tk)
            return kv_ref[0, pl.ds(k0, tk), 256:384]

        o = _softmax_pv(s_ref, qi + 1, mx_ref, slc_v, ls_ref, acc_ref)
        for r in range(hg):
            h = g * hg + r
            og_ref[h * tq:(h + 1) * tq, :] += gate_col(h, 1) * o[r * tq:(r + 1) * tq]

        wcol = 384 + g * 128
        mx = jnp.full((rows, 128), NEG, F32)
        for slot, (back, base) in enumerate(((2, older), (1, None), (0, causal))):
            ktc = jnp.maximum(qi - back, 0)
            k0 = pl.multiple_of(ktc * tk, tk)
            kt_ = kv_ref[0, pl.ds(k0, tk), wcol:wcol + 128]
            sc = _dot_nt(qg, kt_).reshape(hg, tq, tk)
            if back > 0:
                gone = jnp.where(qi >= back, 0.0, NEG)
                madd = gone if base is None else base + gone
            else:
                madd = base
            sc = (sc + madd).reshape(rows, tk)
            s_ref[slot] = sc
            mx = jnp.maximum(mx, _fold_max(sc))
        mx_ref[...] = jnp.broadcast_to(jnp.max(mx, axis=-1, keepdims=True), (rows, 128))

        def win_v(slot):
            ktc = jnp.maximum(qi - 2 + slot, 0)
            k0 = pl.multiple_of(ktc * tk, tk)
            return kv_ref[0, pl.ds(k0, tk), 640:768]

        o = _softmax_pv(s_ref, 3, mx_ref, win_v, ls_ref, acc_ref)
        for r in range(hg):
            h = g * hg + r
            og_ref[h * tq:(h + 1) * tq, :] += gate_col(h, 2) * o[r * tq:(r + 1) * tq]

    for j in range(hg):
        o_ref[0, :, j * 128:(j + 1) * 128] = jnp.where(
            lo, og_ref[j * tq:(j + 1) * tq, :], og_ref[(hg + j) * tq:(hg + j + 1) * tq, :]).astype(BF16)


def _nsa_call(qa, ckv, nkv, gn, qf, ov, e3):
    bsz, s, _ = qa.shape
    tq, tk = TQ, TK
    rows = (N_HEADS // NSA_GROUPS) * tq
    n_cmp_pad = ckv.shape[1]
    return pl.pallas_call(
        _nsa_kernel,
        out_shape=jax.ShapeDtypeStruct((bsz, s, 512), BF16),
        grid=(bsz, s // tq),
        in_specs=[pl.BlockSpec((1, tq, 512), lambda b, i: (b, i, 0)),
                  pl.BlockSpec((1, n_cmp_pad, 384), lambda b, i: (b, 0, 0)),
                  pl.BlockSpec((1, s, 768), lambda b, i: (b, 0, 0)),
                  pl.BlockSpec((1, tq, 128), lambda b, i: (b, i, 0)),
                  _const_spec(qf.shape), _const_spec(ov.shape), _const_spec(e3.shape)],
        out_specs=pl.BlockSpec((1, tq, 512), lambda b, i: (b, i, 0)),
        scratch_shapes=[pltpu.VMEM((N_HEADS * tq, 128), BF16),
                        pltpu.VMEM((s // tk, rows, tk), F32),
                        pltpu.VMEM((rows, 128), F32),
                        pltpu.VMEM((rows, 128), F32),
                        pltpu.VMEM((rows, 128), F32),
                        pltpu.VMEM((N_HEADS * tq, 128), F32),
                        pltpu.VMEM((NSA_GROUPS, tq, 128), BF16)],
        compiler_params=_params("arbitrary", "arbitrary"),
        name="nsa",
    )(qa, ckv, nkv, gn, qf, ov, e3)


def _dsa_kernel(qb_ref, kv_ref, iq_ref, iw_ref, qf_ref, ut_ref, o_ref,
                qop_ref, iop_ref, key_ref, mk_ref, s_ref, mx_ref, ls_ref, acc_ref):
    tq, tk = TQ, TK
    half = N_HEADS // 2
    rows = half * tq
    qi = pl.program_id(1)
    n_kt = qi + 1
    lane = lax.broadcasted_iota(I32, (tq, 128), 1)
    lo = lane < 64
    r2, c2 = _causal_add(tq, tk)
    is_causal = c2 <= r2

    qblk = qb_ref[0]
    iblk = iq_ref[0]
    qf = qf_ref[...]
    zero = jnp.zeros((tq, 128), BF16)
    for j in range(half):
        qb_j = qblk[:, j * 128:(j + 1) * 128]
        iq_j = iblk[:, j * 128:(j + 1) * 128]
        qop_ref[j * tq:(j + 1) * tq, :] = jnp.where(lo, qb_j, qf[2 * j:2 * j + 1, :])
        qop_ref[(half + j) * tq:(half + j + 1) * tq, :] = jnp.where(lo, qf[2 * j + 1:2 * j + 2, :], qb_j)
        iop_ref[j * tq:(j + 1) * tq, :] = jnp.where(lo, iq_j, zero)
        iop_ref[(half + j) * tq:(half + j + 1) * tq, :] = jnp.where(lo, zero, iq_j)

    w = iw_ref[0]

    def idx_body(kt, carry):
        k0 = pl.multiple_of(kt * tk, tk)
        ik = kv_ref[0, pl.ds(k0, tk), 384:512]
        sc = _dot_nt(iop_ref[...], ik).reshape(N_HEADS, tq, tk)
        tot = jnp.zeros((tq, tk), F32)
        for i in range(N_HEADS):
            h = 2 * (i % half) + i // half
            tot = tot + w[:, h:h + 1] * jnp.maximum(sc[i], 0.0)
        past = jnp.where(kt < qi, tk, 0)
        tot = jnp.where(c2 <= r2 + past, tot, -jnp.inf)
        bits = pltpu.bitcast(tot, I32)
        key_ref[kt] = bits ^ ((bits >> 31) & 0x7FFFFFFF)
        return carry

    lax.fori_loop(0, n_kt, idx_body, 0)

    topk = float(min(DSA_TOPK_MAX, kv_ref.shape[1] // 4))

    def count(pred_fn):
        def body(kt, c):
            hit = jnp.where(pred_fn(key_ref[kt]), 1.0, 0.0)
            for cc in range(0, tk, 128):
                c = c + hit[:, cc:cc + 128]
            return c
        c = lax.fori_loop(0, n_kt, body, jnp.zeros((tq, 128), F32))
        return jnp.sum(c, axis=-1, keepdims=True)

    def bit_body(i, prefix):
        bm = jnp.left_shift(jnp.int32(1), 31 - i)
        cand = jnp.broadcast_to(((prefix | bm) ^ INT_MIN), (tq, tk))
        tot = count(lambda k: k >= cand)
        return jnp.where(tot >= topk, prefix | bm, prefix)

    prefix = lax.fori_loop(0, 32, bit_body, jnp.zeros((tq, 1), I32))
    thr = jnp.broadcast_to(prefix ^ INT_MIN, (tq, tk))
    need = topk - count(lambda k: k > thr)

    def mask_body(kt, run):
        k = key_ref[kt]
        eq = k == thr
        eqf = jnp.where(eq, 1.0, 0.0)
        before = _dot(eqf.astype(BF16), ut_ref[...]) + run
        sel = (k > thr) | (eq & (before < need))
        mk_ref[kt] = jnp.where(sel, 0.0, NEG)
        return run + jnp.sum(eqf, axis=-1, keepdims=True)

    lax.fori_loop(0, n_kt, mask_body, jnp.zeros((tq, 1), F32))
    causal = jnp.where(is_causal, 0.0, NEG)

    outs = []
    for part in range(2):
        qh = qop_ref[part * rows:(part + 1) * rows, :]
        kcol = part * 128

        def scores(kt, extra):
            k0 = pl.multiple_of(kt * tk, tk)
            kt_ = kv_ref[0, pl.ds(k0, tk), kcol:kcol + 128]
            sc = _dot_nt(qh, kt_).reshape(half, tq, tk)
            madd = mk_ref[kt]
            if extra is not None:
                madd = madd + extra
            return (sc + madd[None]).reshape(rows, tk)

        mx_ref[...] = jnp.full(mx_ref.shape, NEG, F32)

        def pass_a(kt, carry):
            sc = scores(kt, None)
            s_ref[kt] = sc
            mx_ref[...] = jnp.maximum(mx_ref[...], _fold_max(sc))
            return carry

        lax.fori_loop(0, qi, pass_a, 0)
        sc = scores(qi, causal)
        s_ref[qi] = sc
        mx = jnp.maximum(mx_ref[...], _fold_max(sc))
        mx_ref[...] = jnp.broadcast_to(jnp.max(mx, axis=-1, keepdims=True), (rows, 128))

        def v_tile(kt):
            k0 = pl.multiple_of(kt * tk, tk)
            return kv_ref[0, pl.ds(k0, tk), 256:384]

        outs.append(_softmax_pv(s_ref, n_kt, mx_ref, v_tile, ls_ref, acc_ref))

    for j in range(half):
        o_ref[0, :, j * 128:(j + 1) * 128] = jnp.where(
            lo, outs[0][j * tq:(j + 1) * tq], outs[1][j * tq:(j + 1) * tq]).astype(BF16)


def _dsa_call(qb, dkv, iq, iw, qf, ut):
    bsz, s, _ = qb.shape
    tq, tk = TQ, TK
    rows = (N_HEADS // 2) * tq
    return pl.pallas_call(
        _dsa_kernel,
        out_shape=jax.ShapeDtypeStruct((bsz, s, 512), BF16),
        grid=(bsz, s // tq),
        in_specs=[pl.BlockSpec((1, tq, 512), lambda b, i: (b, i, 0)),
                  pl.BlockSpec((1, s, 512), lambda b, i: (b, 0, 0)),
                  pl.BlockSpec((1, tq, 512), lambda b, i: (b, i, 0)),
                  pl.BlockSpec((1, tq, 128), lambda b, i: (b, i, 0)),
                  _const_spec(qf.shape), _const_spec(ut.shape)],
        out_specs=pl.BlockSpec((1, tq, 512), lambda b, i: (b, i, 0)),
        scratch_shapes=[pltpu.VMEM((N_HEADS * tq, 128), BF16),
                        pltpu.VMEM((N_HEADS * tq, 128), BF16),
                        pltpu.VMEM((s // tk, tq, tk), I32),
                        pltpu.VMEM((s // tk, tq, tk), F32),
                        pltpu.VMEM((s // tk, rows, tk), F32),
                        pltpu.VMEM((rows, 128), F32),
                        pltpu.VMEM((rows, 128), F32),
                        pltpu.VMEM((rows, 128), F32)],
        compiler_params=_params("arbitrary", "arbitrary"),
        name="dsa",
    )(qb, dkv, iq, iw, qf, ut)


_FF_CHUNK = 256


def _post_kernel(x_ref, oa_ref, ob_ref, gate_ref, mod_ref, g2_ref,
                 woa_ref, wob_ref, wout_ref, wg_ref, wu_ref, wd_ref, o_ref, acc_ref):
    d = x_ref.shape[2]
    m = mod_ref[0]
    gate1, shift2, scale2, gate2 = m[2:3], m[3:4], m[4:5], m[5:6]
    ga = _sigmoid(gate_ref[0, :, 0:d].astype(F32))
    gb = _sigmoid(gate_ref[0, :, d:2 * d].astype(F32))
    y = ga * _dot(oa_ref[0], woa_ref[...]) + gb * _dot(ob_ref[0], wob_ref[...])
    x1 = x_ref[0] + gate1 * _dot(y.astype(BF16), wout_ref[...])
    xn = x1 * lax.rsqrt(jnp.mean(x1 * x1, axis=-1, keepdims=True) + RMS_EPS) * g2_ref[...]
    h2 = (xn * (1.0 + scale2) + shift2).astype(BF16)
    acc_ref[...] = jnp.zeros_like(acc_ref)
    d_ff = wg_ref.shape[1]
    for c in range(0, d_ff, _FF_CHUNK):
        gg = _dot(h2, wg_ref[:, c:c + _FF_CHUNK])
        uu = _dot(h2, wu_ref[:, c:c + _FF_CHUNK])
        a = (gg * _sigmoid(gg) * uu).astype(BF16)
        acc_ref[...] += _dot(a, wd_ref[c:c + _FF_CHUNK, :])
    o_ref[0] = x1 + gate2 * acc_ref[...]


def _post_call(x, oa, ob, gates, mod, g2, woa, wob, wout, wg, wu, wd):
    bsz, s, d = x.shape
    tm = TM_PROJ
    weights = (woa, wob, wout, wg, wu, wd)
    w_specs = [pl.BlockSpec(w.shape, lambda b, i: (0, 0), pipeline_mode=pl.Buffered(1)) for w in weights]
    return pl.pallas_call(
        _post_kernel,
        out_shape=jax.ShapeDtypeStruct((bsz, s, d), F32),
        grid=(bsz, s // tm),
        in_specs=[pl.BlockSpec((1, tm, d), lambda b, i: (b, i, 0)),
                  pl.BlockSpec((1, tm, 512), lambda b, i: (b, i, 0)),
                  pl.BlockSpec((1, tm, 512), lambda b, i: (b, i, 0)),
                  pl.BlockSpec((1, tm, 2 * d), lambda b, i: (b, i, 0)),
                  pl.BlockSpec((1, N_ADA, d), lambda b, i: (b, 0, 0)),
                  _const_spec((1, d))] + w_specs,
        out_specs=pl.BlockSpec((1, tm, d), lambda b, i: (b, i, 0)),
        scratch_shapes=[pltpu.VMEM((tm, d), F32)],
        compiler_params=_params("arbitrary", "arbitrary"),
        name="post",
    )(x, oa, ob, gates, mod, g2, *weights)


def _alibi_slopes(n):
    return np.exp2(-8.0 * np.arange(1, n + 1, dtype=np.float64) / n)


def _query_features(low_heads):
    sl = _alibi_slopes(N_HEADS)
    qf = np.zeros((N_HEADS, 128), np.float32)
    for h in range(N_HEADS):
        base = 64 if h in low_heads else 0
        qf[h, base] = 64.0 * sl[h]
        qf[h, base + 1] = sl[h]
    return jnp.asarray(qf, BF16)


def _block_diag_mean(n):
    i = np.arange(n)
    return jnp.asarray((i[:, None] // HEAD_DIM == i[None, :] // HEAD_DIM) / HEAD_DIM, BF16)


def _pad_cols(w, n):
    return jnp.pad(w, ((0, 0), (0, n - w.shape[1])))


def _layer(x, c, w_ada, b_ada, g_norm1, g_norm2, w_in, g_q_a, g_kc_a, g_ks_a, g_kw_a,
           pe_ck, pe_cv, w_ck1, w_ck2, w_cv1, w_cv2, g_q_b, g_k_b, w_o_a, w_o_b, w_out,
           w_ff_gate, w_ff_up, w_ff_down):
    bsz, s, d = x.shape
    hd = HEAD_DIM
    n_slc = s // SLC_LEN

    perm = np.array([(j + 4 * half) * hd + dd for j in range(4) for half in range(2) for dd in range(hd)])

    sizes = (512, 128, 128, 128, 128, 128, 128, 24, 512, 64, 64, 512, 64, 8, d, d)
    offs = np.concatenate([[0], np.cumsum(sizes)])
    col = lambda i: w_in[:, offs[i]:offs[i + 1]]
    dup = lambda w: jnp.concatenate([w, w], axis=1)
    w_all = jnp.concatenate([
        col(0)[:, perm], col(1), col(2), col(3), col(4), col(5), col(6), _pad_cols(col(7), 128),
        col(8), dup(col(9)), dup(col(10)), dup(col(12)), col(11) * (hd ** -0.5),
        _pad_cols(col(13), 128), col(14), col(15)], axis=1).astype(BF16)
    assert w_all.shape[1] == _C_END
    qscale = hd ** -0.5
    gains = jnp.concatenate([jnp.tile(g_q_a, 8) * qscale, jnp.tile(g_ks_a, 2), jnp.tile(g_kw_a, 2),
                             jnp.tile(g_q_b, 8) * qscale, jnp.tile(g_k_b, 2)]).reshape(1, _G_END)
    bd = _block_diag_mean(256)

    mod = _mod_call(c, w_ada, b_ada).reshape(bsz, N_ADA, d)
    qa, kvc, nkv, gn, qb, dkv, iq, iw, gates = _inproj_call(
        x, mod, g_norm1.reshape(1, d), w_all, bd, gains)

    half_len = CMP_LEN // 2

    def cmp_weight(w1k, w1v, first):
        wk = w1k.reshape(CMP_LEN, hd, CMP_HIDDEN)
        wv = w1v.reshape(CMP_LEN, hd, CMP_HIDDEN)
        sl = slice(0, half_len) if first else slice(half_len, CMP_LEN)
        big = jnp.zeros((half_len, 4 * hd, 4 * CMP_HIDDEN), F32)
        for j, wsrc in enumerate((wk, wk, wv, wv)):
            big = big.at[:, j * hd:(j + 1) * hd, j * CMP_HIDDEN:(j + 1) * CMP_HIDDEN].set(wsrc[sl])
        return big.reshape(half_len * 4 * hd, 4 * CMP_HIDDEN).astype(BF16)

    w2 = jnp.zeros((4 * CMP_HIDDEN, 4 * hd), F32)
    for j, wsrc in enumerate((w_ck2, w_ck2, w_cv2, w_cv2)):
        w2 = w2.at[j * CMP_HIDDEN:(j + 1) * CMP_HIDDEN, j * hd:(j + 1) * hd].set(wsrc)
    ckv = _compress_call(
        kvc.reshape(bsz, s // CMP_STRIDE, CMP_STRIDE * 4 * hd),
        cmp_weight(w_ck1, w_cv1, True), cmp_weight(w_ck1, w_cv1, False),
        pe_ck.reshape(1, CMP_LEN * hd), pe_cv.reshape(1, CMP_LEN * hd),
        w_ck1.astype(BF16), w_cv1.astype(BF16), w2.astype(BF16),
        jnp.tile(g_kc_a, 2).reshape(1, 128), _block_diag_mean(128))

    n_cmp_pad = s // CMP_STRIDE
    cs = np.arange(n_cmp_pad)[:, None] * CMP_STRIDE
    jj = np.arange(128)[None, :]
    ov = ((cs < (jj + 1) * SLC_LEN) & (cs + CMP_LEN > jj * SLC_LEN) & (jj < n_slc)
          & (np.arange(n_cmp_pad)[:, None] < (s - CMP_LEN) // CMP_STRIDE + 1))
    kk = np.arange(s)
    e3 = (np.arange(128)[:, None] == (kk // SLC_LEN)[None, :]) * MASK_BIG
    e3 = e3.reshape(128, s // TK, TK).transpose(1, 0, 2)
    o_a = _nsa_call(qa, ckv, nkv, gn, _query_features(range(4)),
                    jnp.asarray(ov, BF16), jnp.asarray(e3, BF16))

    ut = jnp.asarray(np.arange(TK)[:, None] < np.arange(TK)[None, :], BF16)
    o_b = _dsa_call(qb, dkv, iq, iw, _query_features(range(0, N_HEADS, 2)), ut)

    return _post_call(x, o_a, o_b, gates, mod, g_norm2.reshape(1, d),
                      w_o_a[perm].astype(BF16), w_o_b.astype(BF16), w_out.astype(BF16),
                      w_ff_gate.astype(BF16), w_ff_up.astype(BF16), w_ff_down.astype(BF16))


def kernel(x, c, w_ada, b_ada, g_norm1, g_norm2, w_in, g_q_a, g_kc_a, g_ks_a, g_kw_a, pe_ck, pe_cv, w_ck1, w_ck2, w_cv1, w_cv2, g_q_b, g_k_b, w_o_a, w_o_b, w_out, w_ff_gate, w_ff_up, w_ff_down):
    for l in range(w_ada.shape[0]):
        x = _layer(x, c, w_ada[l], b_ada[l], g_norm1[l], g_norm2[l], w_in[l],
                   g_q_a[l], g_kc_a[l], g_ks_a[l], g_kw_a[l], pe_ck[l], pe_cv[l],
                   w_ck1[l], w_ck2[l], w_cv1[l], w_cv2[l], g_q_b[l], g_k_b[l],
                   w_o_a[l], w_o_b[l], w_out[l], w_ff_gate[l], w_ff_up[l], w_ff_down[l])
    return x
```

```python
import numpy as np
import jax
import jax.numpy as jnp
from jax import lax
from jax.experimental import pallas as pl
from jax.experimental.pallas import tpu as pltpu

F32 = jnp.float32
BF16 = jnp.bfloat16
I32 = jnp.int32
I16 = jnp.int16

HEAD_DIM = 64
N_HEADS = 8
NSA_GROUPS = 2
CMP_LEN = 32
CMP_STRIDE = 16
CMP_HIDDEN = 128
SLC_LEN = 64
SLC_TOPN = 16
WINDOW = 512
DSA_TOPK_MAX = 256
N_ADA = 6
RMS_EPS = 1e-6
FORCE_SCORE = 1e9
NEG = -1e30
MASK_BIG = 2.0 ** 100
HALF16 = 1 << 15

TQ = 256
TK = 256
TM_PROJ = 512
VMEM_LIMIT = 56 * 1024 * 1024


def _dot(a, b):
    return jnp.dot(a, b, preferred_element_type=F32)


def _dot_nt(a, b):
    return lax.dot_general(a, b, (((1,), (1,)), ((), ())), preferred_element_type=F32)


def _sigmoid(x):
    return 1.0 / (1.0 + jnp.exp(-x))


def _params(*sem):
    return pltpu.CompilerParams(dimension_semantics=sem, vmem_limit_bytes=VMEM_LIMIT)


def _const_spec(shape):
    nd = len(shape)
    return pl.BlockSpec(shape, lambda *_: (0,) * nd)


def _mod_kernel(c_ref, w_ref, b_ref, o_ref):
    c = c_ref[...]
    a = c * _sigmoid(c)
    o_ref[...] = jnp.dot(a, w_ref[...], preferred_element_type=F32,
                         precision=lax.Precision.HIGHEST) + b_ref[...]


def _mod_call(c, w_ada, b_ada):
    bsz, d = c.shape
    n = w_ada.shape[1]
    tn = 512
    return pl.pallas_call(
        _mod_kernel,
        out_shape=jax.ShapeDtypeStruct((bsz, n), F32),
        grid=(n // tn,),
        in_specs=[pl.BlockSpec((bsz, d), lambda j: (0, 0)),
                  pl.BlockSpec((d, tn), lambda j: (0, j)),
                  pl.BlockSpec((1, tn), lambda j: (0, j))],
        out_specs=pl.BlockSpec((bsz, tn), lambda j: (0, j)),
        compiler_params=_params("arbitrary"),
        name="mod",
    )(c, w_ada, b_ada.reshape(1, n))


_C_QA, _C_KVC, _C_KS, _C_VS, _C_KW, _C_VW = 0, 512, 768, 896, 1024, 1152
_C_GN, _C_QB, _C_KB, _C_VB, _C_IK, _C_IQ, _C_IW, _C_GATE = 1280, 1408, 1920, 2048, 2176, 2304, 2816, 2944
_C_END = 4992
_G_QA, _G_KS, _G_KW, _G_QB, _G_KB, _G_END = 0, 512, 640, 768, 1280, 1408


def _pos_features(pos, lane, base):
    a = (pos >> 6).astype(F32)
    b = (pos & 63).astype(F32)
    return jnp.where(lane == base, a, jnp.where(lane == base + 1, b, 0.0))


def _inproj_kernel(x_ref, mod_ref, g1_ref, w_ref, bd_ref, gain_ref,
                   qa_ref, kvc_ref, nkv_ref, gn_ref, qb_ref, dkv_ref, iq_ref, iw_ref, gate_ref):
    tm = x_ref.shape[1]
    x = x_ref[0]
    m = mod_ref[0]
    shift, scale = m[0:1], m[1:2]
    xn = x * lax.rsqrt(jnp.mean(x * x, axis=-1, keepdims=True) + RMS_EPS) * g1_ref[...]
    h = (xn * (1.0 + scale) + shift).astype(BF16)

    def seg(a, b):
        return _dot(h, w_ref[:, a:b])

    def normed(acc, goff):
        w = acc.shape[1]
        cw = min(w, 256)
        outs = []
        for c in range(0, w, cw):
            a = acc[:, c:c + cw]
            ms = _dot((a * a).astype(BF16), bd_ref[:cw, :cw])
            outs.append(a * lax.rsqrt(ms + RMS_EPS) * gain_ref[:, goff + c:goff + c + cw])
        return outs[0] if len(outs) == 1 else jnp.concatenate(outs, axis=1)

    row = lax.broadcasted_iota(I32, (tm, 128), 0)
    lane = lax.broadcasted_iota(I32, (tm, 128), 1)
    lo = lane < 64
    pos = pl.program_id(1) * tm + row
    feat_hi = _pos_features(pos, lane, 64)
    feat_lo = _pos_features(pos, lane, 0)

    qa_ref[0] = normed(seg(_C_QA, _C_QA + 512), _G_QA).astype(BF16)
    kvc_ref[0] = seg(_C_KVC, _C_KVC + 256)

    ksn = normed(seg(_C_KS, _C_KS + 128), _G_KS)
    nkv_ref[0, :, 0:128] = jnp.where(lo, ksn, feat_hi).astype(BF16)
    nkv_ref[0, :, 128:256] = jnp.where(lo, feat_lo, ksn).astype(BF16)
    nkv_ref[0, :, 256:384] = seg(_C_VS, _C_VS + 128).astype(BF16)
    kwn = normed(seg(_C_KW, _C_KW + 128), _G_KW)
    nkv_ref[0, :, 384:512] = jnp.where(lo, kwn, feat_hi).astype(BF16)
    nkv_ref[0, :, 512:640] = jnp.where(lo, feat_lo, kwn).astype(BF16)
    nkv_ref[0, :, 640:768] = seg(_C_VW, _C_VW + 128).astype(BF16)

    gn_ref[0] = seg(_C_GN, _C_GN + 128)

    qb_ref[0] = normed(seg(_C_QB, _C_QB + 512), _G_QB).astype(BF16)
    kbn = normed(seg(_C_KB, _C_KB + 128), _G_KB)
    dkv_ref[0, :, 0:128] = jnp.where(lo, kbn, feat_hi).astype(BF16)
    dkv_ref[0, :, 128:256] = jnp.where(lo, feat_lo, kbn).astype(BF16)
    dkv_ref[0, :, 256:384] = seg(_C_VB, _C_VB + 128).astype(BF16)
    dkv_ref[0, :, 384:512] = seg(_C_IK, _C_IK + 128).astype(BF16)

    iq_ref[0] = seg(_C_IQ, _C_IQ + 512).astype(BF16)
    iw_ref[0] = seg(_C_IW, _C_IW + 128) * (N_HEADS ** -0.5)
    for c in range(0, 2048, 512):
        gate_ref[0, :, c:c + 512] = seg(_C_GATE + c, _C_GATE + c + 512).astype(BF16)


def _inproj_call(x, mod, g1, w_all, bd, gains):
    bsz, s, d = x.shape
    tm = TM_PROJ
    widths = (512, 256, 768, 128, 512, 512, 512, 128, 2048)
    dtypes = (BF16, F32, BF16, F32, BF16, BF16, BF16, F32, BF16)
    return pl.pallas_call(
        _inproj_kernel,
        out_shape=[jax.ShapeDtypeStruct((bsz, s, w), dt) for w, dt in zip(widths, dtypes)],
        grid=(bsz, s // tm),
        in_specs=[pl.BlockSpec((1, tm, d), lambda b, i: (b, i, 0)),
                  pl.BlockSpec((1, N_ADA, d), lambda b, i: (b, 0, 0)),
                  _const_spec((1, d)),
                  _const_spec(w_all.shape),
                  _const_spec(bd.shape),
                  _const_spec(gains.shape)],
        out_specs=[pl.BlockSpec((1, tm, w), lambda b, i: (b, i, 0)) for w in widths],
        compiler_params=_params("arbitrary", "arbitrary"),
        name="inproj",
    )(x, mod, g1, w_all, bd, gains)


def _compress_kernel(t_ref, wa_ref, wb_ref, pek_ref, pev_ref, w1k_ref, w1v_ref, w2_ref,
                     gkc_ref, bd_ref, o_ref):
    n = t_ref.shape[1]
    t = t_ref[0].astype(BF16)
    first = _dot(t, wa_ref[...])
    second = _dot(t, wb_ref[...])
    second = pltpu.roll(second, n - 1, 0)
    ck = _dot(jnp.broadcast_to(pek_ref[...], (8, pek_ref.shape[1])).astype(BF16), w1k_ref[...])[0:1]
    cv = _dot(jnp.broadcast_to(pev_ref[...], (8, pev_ref.shape[1])).astype(BF16), w1v_ref[...])[0:1]
    hid = first + second + jnp.concatenate([ck, ck, cv, cv], axis=1)
    act = hid * _sigmoid(hid)
    out = _dot(act.astype(BF16), w2_ref[...])
    kc = out[:, 0:128]
    ms = _dot((kc * kc).astype(BF16), bd_ref[...])
    kcn = kc * lax.rsqrt(ms + RMS_EPS) * gkc_ref[...]
    row = lax.broadcasted_iota(I32, (n, 128), 0)
    lane = lax.broadcasted_iota(I32, (n, 128), 1)
    lo = lane < 64
    cpos = row * CMP_STRIDE + (CMP_LEN - 1)
    o_ref[0, :, 0:128] = jnp.where(lo, kcn, _pos_features(cpos, lane, 64)).astype(BF16)
    o_ref[0, :, 128:256] = jnp.where(lo, _pos_features(cpos, lane, 0), kcn).astype(BF16)
    o_ref[0, :, 256:384] = out[:, 128:256].astype(BF16)


def _compress_call(t16, wa, wb, pek, pev, w1k, w1v, w2, gkc, bd):
    bsz, n, k = t16.shape
    args = (wa, wb, pek, pev, w1k, w1v, w2, gkc, bd)
    return pl.pallas_call(
        _compress_kernel,
        out_shape=jax.ShapeDtypeStruct((bsz, n, 384), BF16),
        grid=(bsz,),
        in_specs=[pl.BlockSpec((1, n, k), lambda b: (b, 0, 0))] + [_const_spec(a.shape) for a in args],
        out_specs=pl.BlockSpec((1, n, 384), lambda b: (b, 0, 0)),
        compiler_params=_params("arbitrary"),
        name="compress",
    )(t16, *args)


def _iota2(shape):
    return lax.broadcasted_iota(I32, shape, 0), lax.broadcasted_iota(I32, shape, 1)


def _for_tiles(n, tile_fn):
    def two(p, carry):
        tile_fn(2 * p)
        tile_fn(2 * p + 1)
        return carry

    lax.fori_loop(0, n >> 1, two, 0)

    @pl.when((n & 1) == 1)
    def _():
        tile_fn(n - 1)


def _fold_max(s):
    out = s[:, 0:128]
    for c in range(128, s.shape[1], 128):
        out = jnp.maximum(out, s[:, c:c + 128])
    return out


def _finish_max(m_ref, g, last):
    mx = jnp.maximum(m_ref[g], _fold_max(last))
    m_ref[g] = jnp.broadcast_to(jnp.max(mx, axis=-1, keepdims=True), mx.shape)


def _softmax_pv(s_ref, m_ref, ls_ref, acc_ref, n_tiles, v_tile, unrolled=False):
    n_streams = s_ref.shape[0]
    ls_ref[...] = jnp.zeros_like(ls_ref)
    acc_ref[...] = jnp.zeros_like(acc_ref)

    def body(kt, carry):
        for g in range(n_streams):
            s = s_ref[g, kt]
            m_rep = m_ref[g]
            halves = [jnp.exp(s[:, c:c + 128] - m_rep) for c in range(0, s.shape[1], 128)]
            tot = halves[0]
            for hv in halves[1:]:
                tot = tot + hv
            ls_ref[g] += tot
            p = jnp.concatenate(halves, axis=1).astype(BF16)
            acc_ref[g] += _dot(p, v_tile(g, kt))
        return carry

    if unrolled:
        for kt in range(n_tiles):
            body(kt, 0)
    else:
        _for_tiles(n_tiles, lambda kt: body(kt, 0))
    outs = []
    for g in range(n_streams):
        l = jnp.sum(ls_ref[g], axis=-1, keepdims=True)
        outs.append(acc_ref[g] / jnp.maximum(l, 1e-30))
    return outs


def _nsa_kernel(qa_ref, ckv_ref, kv_ref, gn_ref, qf_ref, ovt_ref, e3_ref, o_ref,
                qop_ref, s_ref, mx_ref, ls_ref, acc_ref, og_ref, selm_ref, imp_ref, keep_ref):
    tq, tk = TQ, TK
    ng = NSA_GROUPS
    hg = N_HEADS // ng
    rows = hg * tq
    n_slc = e3_ref.shape[0] * tk // SLC_LEN
    qi = pl.program_id(1)
    q0 = qi * tq
    row, lane = _iota2((tq, 128))
    lo = lane < 64
    pos = q0 + row

    qblk = qa_ref[0]
    qf = qf_ref[...]
    for j in range(hg):
        blk = qblk[:, j * 128:(j + 1) * 128]
        qop_ref[j * tq:(j + 1) * tq, :] = jnp.where(lo, blk, qf[j:j + 1, :])
        qop_ref[(hg + j) * tq:(hg + j + 1) * tq, :] = jnp.where(lo, qf[hg + j:hg + j + 1, :], blk)

    gt = _sigmoid(gn_ref[0])

    def add_gated(outs, br, first=False):
        for g in range(ng):
            for r in range(hg):
                h = g * hg + r
                val = gt[:, 3 * h + br:3 * h + br + 1] * outs[g][r * tq:(r + 1) * tq]
                if first:
                    og_ref[h * tq:(h + 1) * tq, :] = val
                else:
                    og_ref[h * tq:(h + 1) * tq, :] += val

    def q_of(g):
        return qop_ref[g * rows:(g + 1) * rows, :]

    r2, c2 = _iota2((tq, tk))
    causal = jnp.where(c2 <= r2, 0.0, NEG)
    older = jnp.where(c2 > r2, 0.0, NEG)

    seen = (lane * CMP_STRIDE + (CMP_LEN - 1)) <= pos
    seen_add = jnp.where(seen, 0.0, NEG)
    seen_mul = jnp.where(seen, 1.0, 0.0)
    keep_ref[...] = jnp.zeros_like(keep_ref)
    jrow, tcol = _iota2((n_slc, tq))
    post = q0 + tcol
    cur = post >> 6
    forced = (jrow == 0) | (jrow == cur) | (jrow == cur - 1)
    visible = (jrow * SLC_LEN) <= post
    cmp_outs = []
    for g in range(ng):
        kc = ckv_ref[0, :, g * 128:(g + 1) * 128]
        vc = ckv_ref[0, :, 256:384]
        sm = (_dot_nt(q_of(g), kc).reshape(hg, tq, 128) + seen_add[None]).reshape(rows, 128)
        m = jnp.max(sm, axis=-1, keepdims=True)
        e = (jnp.exp(sm - m).reshape(hg, tq, 128) * seen_mul[None]).reshape(rows, 128)
        p = e / jnp.maximum(jnp.sum(e, axis=-1, keepdims=True), 1e-30)
        cmp_outs.append(_dot(p.astype(BF16), vc))

        psum = p[0:tq]
        for r in range(1, hg):
            psum = psum + p[r * tq:(r + 1) * tq]
        p_hi = psum.astype(BF16)
        p_lo = (psum - p_hi.astype(F32)).astype(BF16)
        imp = _dot_nt(ovt_ref[...], p_hi) + _dot_nt(ovt_ref[...], p_lo)
        impf = jnp.where(forced, FORCE_SCORE, jnp.where(visible, imp, -FORCE_SCORE))
        imp_ref[...] = impf
        rank = jnp.zeros((n_slc, tq), F32)
        for i in range(n_slc):
            other = imp_ref[i:i + 1, :]
            rank = rank + jnp.where(other > impf, 1.0, 0.0) + jnp.where((other == impf) & (jrow > i), 1.0, 0.0)
        keep_ref[0:n_slc, :] = jnp.where(rank < float(SLC_TOPN), 0.0, -1.0)
        selm_ref[g] = keep_ref[...].T.astype(BF16)
    add_gated(cmp_outs, 0, first=True)

    def slc_scores(g, kt, extra):
        k0 = pl.multiple_of(kt * tk, tk)
        kt_ = kv_ref[0, pl.ds(k0, tk), g * 128:(g + 1) * 128]
        sc = _dot_nt(q_of(g), kt_).reshape(hg, tq, tk)
        madd = _dot(selm_ref[g], e3_ref[kt])
        if extra is not None:
            madd = madd + extra
        return (sc + madd[None]).reshape(rows, tk)

    mx_ref[...] = jnp.full(mx_ref.shape, NEG, F32)

    def slc_a(kt, carry):
        for g in range(ng):
            sc = slc_scores(g, kt, None)
            s_ref[g, kt] = sc
            mx_ref[g] = jnp.maximum(mx_ref[g], _fold_max(sc))
        return carry

    _for_tiles(qi, lambda kt: slc_a(kt, 0))
    for g in range(ng):
        sc = slc_scores(g, qi, causal)
        s_ref[g, qi] = sc
        _finish_max(mx_ref, g, sc)

    def slc_v(g, kt):
        k0 = pl.multiple_of(kt * tk, tk)
        return kv_ref[0, pl.ds(k0, tk), 256:384]

    add_gated(_softmax_pv(s_ref, mx_ref, ls_ref, acc_ref, qi + 1, slc_v), 1)

    def win_k0(slot):
        return pl.multiple_of(jnp.maximum(qi - 2 + slot, 0) * tk, tk)

    for g in range(ng):
        mx_ref[g] = jnp.full((rows, 128), NEG, F32)
        for slot, base in enumerate((older, None, causal)):
            back = 2 - slot
            kt_ = kv_ref[0, pl.ds(win_k0(slot), tk), 384 + g * 128:512 + g * 128]
            sc = _dot_nt(q_of(g), kt_).reshape(hg, tq, tk)
            if back > 0:
                gone = jnp.where(qi >= back, 0.0, NEG)
                madd = gone if base is None else base + gone
            else:
                madd = base
            sc = (sc + madd).reshape(rows, tk)
            s_ref[g, slot] = sc
            if slot < 2:
                mx_ref[g] = jnp.maximum(mx_ref[g], _fold_max(sc))
            else:
                _finish_max(mx_ref, g, sc)

    def win_v(g, slot):
        return kv_ref[0, pl.ds(win_k0(slot), tk), 640:768]

    add_gated(_softmax_pv(s_ref, mx_ref, ls_ref, acc_ref, 3, win_v, unrolled=True), 2)

    for j in range(hg):
        o_ref[0, :, j * 128:(j + 1) * 128] = jnp.where(
            lo, og_ref[j * tq:(j + 1) * tq, :], og_ref[(hg + j) * tq:(hg + j + 1) * tq, :]).astype(BF16)


def _nsa_call(qa, ckv, nkv, gn, qf, ovt, e3):
    bsz, s, _ = qa.shape
    tq, tk = TQ, TK
    ng = NSA_GROUPS
    rows = (N_HEADS // ng) * tq
    n_cmp_pad = ckv.shape[1]
    return pl.pallas_call(
        _nsa_kernel,
        out_shape=jax.ShapeDtypeStruct((bsz, s, 512), BF16),
        grid=(bsz, s // tq),
        in_specs=[pl.BlockSpec((1, tq, 512), lambda b, i: (b, i, 0)),
                  pl.BlockSpec((1, n_cmp_pad, 384), lambda b, i: (b, 0, 0)),
                  pl.BlockSpec((1, s, 768), lambda b, i: (b, 0, 0)),
                  pl.BlockSpec((1, tq, 128), lambda b, i: (b, i, 0)),
                  _const_spec(qf.shape), _const_spec(ovt.shape), _const_spec(e3.shape)],
        out_specs=pl.BlockSpec((1, tq, 512), lambda b, i: (b, i, 0)),
        scratch_shapes=[pltpu.VMEM((N_HEADS * tq, 128), BF16),
                        pltpu.VMEM((ng, s // tk, rows, tk), F32),
                        pltpu.VMEM((ng, rows, 128), F32),
                        pltpu.VMEM((ng, rows, 128), F32),
                        pltpu.VMEM((ng, rows, 128), F32),
                        pltpu.VMEM((N_HEADS * tq, 128), F32),
                        pltpu.VMEM((ng, tq, 128), BF16),
                        pltpu.VMEM((ovt.shape[0], tq), F32),
                        pltpu.VMEM((128, tq), F32)],
        compiler_params=_params("arbitrary", "arbitrary"),
        name="nsa",
    )(qa, ckv, nkv, gn, qf, ovt, e3)


def _dsa_kernel(qb_ref, kv_ref, iq_ref, iw_ref, qf_ref, lt_ref, o_ref,
                qop_ref, iqt_ref, key_ref, khi_ref, klo_ref, mk_ref, s_ref, mx_ref, ls_ref, acc_ref):
    tq, tk = TQ, TK
    half = N_HEADS // 2
    rows = half * tq
    qi = pl.program_id(1)
    n_kt = qi + 1
    _, lane = _iota2((tq, 128))
    lo = lane < 64
    r2, c2 = _iota2((tq, tk))
    causal = jnp.where(c2 <= r2, 0.0, NEG)

    qblk = qb_ref[0]
    qf = qf_ref[...]
    for j in range(half):
        qb_j = qblk[:, j * 128:(j + 1) * 128]
        qop_ref[j * tq:(j + 1) * tq, :] = jnp.where(lo, qb_j, qf[2 * j:2 * j + 1, :])
        qop_ref[(half + j) * tq:(half + j + 1) * tq, :] = jnp.where(lo, qf[2 * j + 1:2 * j + 2, :], qb_j)

    iqt_ref[...] = iq_ref[0].astype(F32).T.astype(BF16)
    w_t = iw_ref[0].T[0:N_HEADS]
    kidx, qidx = _iota2((tk, tq))

    def idx_body(kt, carry):
        k0 = pl.multiple_of(kt * tk, tk)
        ik = kv_ref[0, pl.ds(k0, tk), 384:384 + HEAD_DIM]
        tot = jnp.zeros((tk, tq), F32)
        for h in range(N_HEADS):
            sc = _dot(ik, iqt_ref[h * HEAD_DIM:(h + 1) * HEAD_DIM, :])
            tot = tot + w_t[h:h + 1, :] * jnp.maximum(sc, 0.0)
        past = jnp.where(kt < qi, tk, 0)
        tot = jnp.where(kidx <= qidx + past, tot, -jnp.inf)
        bits = pltpu.bitcast(tot, I32)
        key = bits ^ ((bits >> 31) & 0x7FFFFFFF)
        key_ref[kt] = key
        khi_ref[kt] = (key >> 16).astype(I16)
        klo_ref[kt] = ((key & 0xFFFF) - HALF16).astype(I16)
        return carry

    lax.fori_loop(0, n_kt, idx_body, 0)

    topk = float(min(DSA_TOPK_MAX, kv_ref.shape[1] // 4))

    def count16(ref, pred_fn):
        def body(kt, c):
            hit = jnp.where(pred_fn(ref[kt]), jnp.bfloat16(1), jnp.bfloat16(0))
            parts = [hit[r * 16:(r + 1) * 16] for r in range(tk // 16)]
            while len(parts) > 1:
                parts = [parts[i] + parts[i + 1] for i in range(0, len(parts), 2)]
            return c + parts[0].astype(F32)
        c = lax.fori_loop(0, n_kt, body, jnp.zeros((16, tq), F32))
        return jnp.sum(c, axis=0, keepdims=True)

    def kth_half(ref, kth):
        def bit_body(i, prefix):
            bm = jnp.left_shift(jnp.int32(1), 15 - i)
            cand = ((prefix | bm) - HALF16).astype(I16)
            tot = count16(ref, lambda h: h >= cand)
            return jnp.where(tot >= kth, prefix | bm, prefix)
        return lax.fori_loop(0, 16, bit_body, jnp.zeros((1, tq), I32))

    hi_u = kth_half(khi_ref, topk)
    hi16 = (hi_u - HALF16).astype(I16)
    left = topk - count16(khi_ref, lambda h: h > hi16)

    def bucket_body(kt, carry):
        klo_ref[kt] = jnp.where(khi_ref[kt] == hi16, klo_ref[kt], jnp.int16(-HALF16))
        return carry

    lax.fori_loop(0, n_kt, bucket_body, 0)
    lo_u = kth_half(klo_ref, left)
    thr = ((hi_u - HALF16) << 16) | lo_u

    def count_gt(kt, c):
        hit = jnp.where(key_ref[kt] > thr, 1.0, 0.0)
        return c + jnp.sum(hit.reshape(tk // 8, 8, tq), axis=0)

    c_gt = lax.fori_loop(0, n_kt, count_gt, jnp.zeros((8, tq), F32))
    need = topk - jnp.sum(c_gt, axis=0, keepdims=True)

    def mask_body(kt, run):
        k = key_ref[kt]
        eq = k == thr
        eqf = jnp.where(eq, 1.0, 0.0)
        before = _dot(lt_ref[...], eqf.astype(BF16)) + run
        tie = jnp.where(before < need, 0.0, NEG)
        m_t = jnp.where(k > thr, 0.0, jnp.where(eq, tie, NEG))
        mk_ref[kt] = m_t.T
        return run + jnp.sum(eqf, axis=0, keepdims=True)

    lax.fori_loop(0, n_kt, mask_body, jnp.zeros((1, tq), F32))

    def scores(g, kt, extra):
        k0 = pl.multiple_of(kt * tk, tk)
        kt_ = kv_ref[0, pl.ds(k0, tk), g * 128:(g + 1) * 128]
        sc = _dot_nt(qop_ref[g * rows:(g + 1) * rows, :], kt_).reshape(half, tq, tk)
        madd = mk_ref[kt]
        if extra is not None:
            madd = madd + extra
        return (sc + madd[None]).reshape(rows, tk)

    mx_ref[...] = jnp.full(mx_ref.shape, NEG, F32)

    def pass_a(kt, carry):
        for g in range(2):
            sc = scores(g, kt, None)
            s_ref[g, kt] = sc
            mx_ref[g] = jnp.maximum(mx_ref[g], _fold_max(sc))
        return carry

    _for_tiles(qi, lambda kt: pass_a(kt, 0))
    for g in range(2):
        sc = scores(g, qi, causal)
        s_ref[g, qi] = sc
        _finish_max(mx_ref, g, sc)

    def v_tile(g, kt):
        k0 = pl.multiple_of(kt * tk, tk)
        return kv_ref[0, pl.ds(k0, tk), 256:384]

    outs = _softmax_pv(s_ref, mx_ref, ls_ref, acc_ref, n_kt, v_tile)
    for j in range(half):
        o_ref[0, :, j * 128:(j + 1) * 128] = jnp.where(
            lo, outs[0][j * tq:(j + 1) * tq], outs[1][j * tq:(j + 1) * tq]).astype(BF16)


def _dsa_call(qb, dkv, iq, iw, qf, lt):
    bsz, s, _ = qb.shape
    tq, tk = TQ, TK
    rows = (N_HEADS // 2) * tq
    return pl.pallas_call(
        _dsa_kernel,
        out_shape=jax.ShapeDtypeStruct((bsz, s, 512), BF16),
        grid=(bsz, s // tq),
        in_specs=[pl.BlockSpec((1, tq, 512), lambda b, i: (b, i, 0)),
                  pl.BlockSpec((1, s, 512), lambda b, i: (b, 0, 0)),
                  pl.BlockSpec((1, tq, 512), lambda b, i: (b, i, 0)),
                  pl.BlockSpec((1, tq, 128), lambda b, i: (b, i, 0)),
                  _const_spec(qf.shape), _const_spec(lt.shape)],
        out_specs=pl.BlockSpec((1, tq, 512), lambda b, i: (b, i, 0)),
        scratch_shapes=[pltpu.VMEM((N_HEADS * tq, 128), BF16),
                        pltpu.VMEM((N_HEADS * HEAD_DIM, tq), BF16),
                        pltpu.VMEM((s // tk, tk, tq), I32),
                        pltpu.VMEM((s // tk, tk, tq), I16),
                        pltpu.VMEM((s // tk, tk, tq), I16),
                        pltpu.VMEM((s // tk, tq, tk), F32),
                        pltpu.VMEM((2, s // tk, rows, tk), F32),
                        pltpu.VMEM((2, rows, 128), F32),
                        pltpu.VMEM((2, rows, 128), F32),
                        pltpu.VMEM((2, rows, 128), F32)],
        compiler_params=_params("arbitrary", "arbitrary"),
        name="dsa",
    )(qb, dkv, iq, iw, qf, lt)


_FF_CHUNK = 256


def _post_kernel(x_ref, oa_ref, ob_ref, gate_ref, mod_ref, g2_ref,
                 woa_ref, wob_ref, wout_ref, wg_ref, wu_ref, wd_ref, o_ref, acc_ref):
    d = x_ref.shape[2]
    m = mod_ref[0]
    gate1, shift2, scale2, gate2 = m[2:3], m[3:4], m[4:5], m[5:6]
    ga = _sigmoid(gate_ref[0, :, 0:d].astype(F32))
    gb = _sigmoid(gate_ref[0, :, d:2 * d].astype(F32))
    y = ga * _dot(oa_ref[0], woa_ref[...]) + gb * _dot(ob_ref[0], wob_ref[...])
    x1 = x_ref[0] + gate1 * _dot(y.astype(BF16), wout_ref[...])
    xn = x1 * lax.rsqrt(jnp.mean(x1 * x1, axis=-1, keepdims=True) + RMS_EPS) * g2_ref[...]
    h2 = (xn * (1.0 + scale2) + shift2).astype(BF16)
    acc_ref[...] = jnp.zeros_like(acc_ref)
    d_ff = wg_ref.shape[1]
    for c in range(0, d_ff, _FF_CHUNK):
        gg = _dot(h2, wg_ref[:, c:c + _FF_CHUNK])
        uu = _dot(h2, wu_ref[:, c:c + _FF_CHUNK])
        a = (gg * _sigmoid(gg) * uu).astype(BF16)
        acc_ref[...] += _dot(a, wd_ref[c:c + _FF_CHUNK, :])
    o_ref[0] = x1 + gate2 * acc_ref[...]


def _post_call(x, oa, ob, gates, mod, g2, woa, wob, wout, wg, wu, wd):
    bsz, s, d = x.shape
    tm = TM_PROJ
    weights = (woa, wob, wout, wg, wu, wd)
    w_specs = [pl.BlockSpec(w.shape, lambda b, i: (0, 0), pipeline_mode=pl.Buffered(1)) for w in weights]
    return pl.pallas_call(
        _post_kernel,
        out_shape=jax.ShapeDtypeStruct((bsz, s, d), F32),
        grid=(bsz, s // tm),
        in_specs=[pl.BlockSpec((1, tm, d), lambda b, i: (b, i, 0)),
                  pl.BlockSpec((1, tm, 512), lambda b, i: (b, i, 0)),
                  pl.BlockSpec((1, tm, 512), lambda b, i: (b, i, 0)),
                  pl.BlockSpec((1, tm, 2 * d), lambda b, i: (b, i, 0)),
                  pl.BlockSpec((1, N_ADA, d), lambda b, i: (b, 0, 0)),
                  _const_spec((1, d))] + w_specs,
        out_specs=pl.BlockSpec((1, tm, d), lambda b, i: (b, i, 0)),
        scratch_shapes=[pltpu.VMEM((tm, d), F32)],
        compiler_params=_params("arbitrary", "arbitrary"),
        name="post",
    )(x, oa, ob, gates, mod, g2, *weights)


def _alibi_slopes(n):
    return np.exp2(-8.0 * np.arange(1, n + 1, dtype=np.float64) / n)


def _query_features(low_heads):
    sl = _alibi_slopes(N_HEADS)
    qf = np.zeros((N_HEADS, 128), np.float32)
    for h in range(N_HEADS):
        base = 64 if h in low_heads else 0
        qf[h, base] = 64.0 * sl[h]
        qf[h, base + 1] = sl[h]
    return jnp.asarray(qf, BF16)


def _block_diag_mean(n):
    i = np.arange(n)
    return jnp.asarray((i[:, None] // HEAD_DIM == i[None, :] // HEAD_DIM) / HEAD_DIM, BF16)


def _pad_cols(w, n):
    return jnp.pad(w, ((0, 0), (0, n - w.shape[1])))


def _layer(x, c, w_ada, b_ada, g_norm1, g_norm2, w_in, g_q_a, g_kc_a, g_ks_a, g_kw_a,
           pe_ck, pe_cv, w_ck1, w_ck2, w_cv1, w_cv2, g_q_b, g_k_b, w_o_a, w_o_b, w_out,
           w_ff_gate, w_ff_up, w_ff_down):
    bsz, s, d = x.shape
    hd = HEAD_DIM
    n_slc = s // SLC_LEN

    perm = np.array([(j + 4 * half) * hd + dd for j in range(4) for half in range(2) for dd in range(hd)])

    sizes = (512, 128, 128, 128, 128, 128, 128, 24, 512, 64, 64, 512, 64, 8, d, d)
    offs = np.concatenate([[0], np.cumsum(sizes)])
    col = lambda i: w_in[:, offs[i]:offs[i + 1]]
    dup = lambda w: jnp.concatenate([w, w], axis=1)
    w_all = jnp.concatenate([
        col(0)[:, perm], col(1), col(2), col(3), col(4), col(5), col(6), _pad_cols(col(7), 128),
        col(8), dup(col(9)), dup(col(10)), dup(col(12)), col(11) * (hd ** -0.5),
        _pad_cols(col(13), 128), col(14), col(15)], axis=1).astype(BF16)
    assert w_all.shape[1] == _C_END
    qscale = hd ** -0.5
    gains = jnp.concatenate([jnp.tile(g_q_a, 8) * qscale, jnp.tile(g_ks_a, 2), jnp.tile(g_kw_a, 2),
                             jnp.tile(g_q_b, 8) * qscale, jnp.tile(g_k_b, 2)]).reshape(1, _G_END)
    bd = _block_diag_mean(256)

    mod = _mod_call(c, w_ada, b_ada).reshape(bsz, N_ADA, d)
    qa, kvc, nkv, gn, qb, dkv, iq, iw, gates = _inproj_call(
        x, mod, g_norm1.reshape(1, d), w_all, bd, gains)

    half_len = CMP_LEN // 2

    def cmp_weight(w1k, w1v, first):
        wk = w1k.reshape(CMP_LEN, hd, CMP_HIDDEN)
        wv = w1v.reshape(CMP_LEN, hd, CMP_HIDDEN)
        sl = slice(0, half_len) if first else slice(half_len, CMP_LEN)
        big = jnp.zeros((half_len, 4 * hd, 4 * CMP_HIDDEN), F32)
        for j, wsrc in enumerate((wk, wk, wv, wv)):
            big = big.at[:, j * hd:(j + 1) * hd, j * CMP_HIDDEN:(j + 1) * CMP_HIDDEN].set(wsrc[sl])
        return big.reshape(half_len * 4 * hd, 4 * CMP_HIDDEN).astype(BF16)

    w2 = jnp.zeros((4 * CMP_HIDDEN, 4 * hd), F32)
    for j, wsrc in enumerate((w_ck2, w_ck2, w_cv2, w_cv2)):
        w2 = w2.at[j * CMP_HIDDEN:(j + 1) * CMP_HIDDEN, j * hd:(j + 1) * hd].set(wsrc)
    ckv = _compress_call(
        kvc.reshape(bsz, s // CMP_STRIDE, CMP_STRIDE * 4 * hd),
        cmp_weight(w_ck1, w_cv1, True), cmp_weight(w_ck1, w_cv1, False),
        pe_ck.reshape(1, CMP_LEN * hd), pe_cv.reshape(1, CMP_LEN * hd),
        w_ck1.astype(BF16), w_cv1.astype(BF16), w2.astype(BF16),
        jnp.tile(g_kc_a, 2).reshape(1, 128), _block_diag_mean(128))

    n_cmp_pad = s // CMP_STRIDE
    cs = np.arange(n_cmp_pad)[:, None] * CMP_STRIDE
    jj = np.arange(128)[None, :]
    ov = ((cs < (jj + 1) * SLC_LEN) & (cs + CMP_LEN > jj * SLC_LEN) & (jj < n_slc)
          & (np.arange(n_cmp_pad)[:, None] < (s - CMP_LEN) // CMP_STRIDE + 1))
    kk = np.arange(s)
    e3 = (np.arange(128)[:, None] == (kk // SLC_LEN)[None, :]) * MASK_BIG
    e3 = e3.reshape(128, s // TK, TK).transpose(1, 0, 2)
    o_a = _nsa_call(qa, ckv, nkv, gn, _query_features(range(4)),
                    jnp.asarray(ov.T[:n_slc], BF16), jnp.asarray(e3, BF16))

    lt = jnp.asarray(np.arange(TK)[None, :] < np.arange(TK)[:, None], BF16)
    o_b = _dsa_call(qb, dkv, iq, iw, _query_features(range(0, N_HEADS, 2)), lt)

    return _post_call(x, o_a, o_b, gates, mod, g_norm2.reshape(1, d),
                      w_o_a[perm].astype(BF16), w_o_b.astype(BF16), w_out.astype(BF16),
                      w_ff_gate.astype(BF16), w_ff_up.astype(BF16), w_ff_down.astype(BF16))


def kernel(x, c, w_ada, b_ada, g_norm1, g_norm2, w_in, g_q_a, g_kc_a, g_ks_a, g_kw_a, pe_ck, pe_cv, w_ck1, w_ck2, w_cv1, w_cv2, g_q_b, g_k_b, w_o_a, w_o_b, w_out, w_ff_gate, w_ff_up, w_ff_down):
    for l in range(w_ada.shape[0]):
        x = _layer(x, c, w_ada[l], b_ada[l], g_norm1[l], g_norm2[l], w_in[l],
                   g_q_a[l], g_kc_a[l], g_ks_a[l], g_kw_a[l], pe_ck[l], pe_cv[l],
                   w_ck1[l], w_ck2[l], w_cv1[l], w_cv2[l], g_q_b[l], g_k_b[l],
                   w_o_a[l], w_o_b[l], w_out[l], w_ff_gate[l], w_ff_up[l], w_ff_down[l])
    return x
```

```python
import numpy as np
import jax
import jax.numpy as jnp
from jax import lax
from jax.experimental import pallas as pl
from jax.experimental.pallas import tpu as pltpu

F32 = jnp.float32
BF16 = jnp.bfloat16
I32 = jnp.int32
I16 = jnp.int16

HEAD_DIM = 64
N_HEADS = 8
NSA_GROUPS = 2
CMP_LEN = 32
CMP_STRIDE = 16
CMP_HIDDEN = 128
SLC_LEN = 64
SLC_TOPN = 16
WINDOW = 512
DSA_TOPK_MAX = 256
N_ADA = 6
RMS_EPS = 1e-6
FORCE_SCORE = 1e9
NEG = -1e30
MASK_BIG = 2.0 ** 100
HALF16 = 1 << 15

TQ = 256
TK = 256
TM_PROJ = 512
VMEM_LIMIT = 56 * 1024 * 1024


def _dot(a, b):
    return jnp.dot(a, b, preferred_element_type=F32)


def _dot_nt(a, b):
    return lax.dot_general(a, b, (((1,), (1,)), ((), ())), preferred_element_type=F32)


def _sigmoid(x):
    return 1.0 / (1.0 + jnp.exp(-x))


def _params(*sem):
    return pltpu.CompilerParams(dimension_semantics=sem, vmem_limit_bytes=VMEM_LIMIT)


def _const_spec(shape):
    nd = len(shape)
    return pl.BlockSpec(shape, lambda *_: (0,) * nd)


def _mod_kernel(c_ref, w_ref, b_ref, o_ref):
    c = c_ref[...]
    a = c * _sigmoid(c)
    o_ref[...] = jnp.dot(a, w_ref[...], preferred_element_type=F32,
                         precision=lax.Precision.HIGHEST) + b_ref[...]


def _mod_call(c, w_ada, b_ada):
    bsz, d = c.shape
    n = w_ada.shape[1]
    tn = 512
    return pl.pallas_call(
        _mod_kernel,
        out_shape=jax.ShapeDtypeStruct((bsz, n), F32),
        grid=(n // tn,),
        in_specs=[pl.BlockSpec((bsz, d), lambda j: (0, 0)),
                  pl.BlockSpec((d, tn), lambda j: (0, j)),
                  pl.BlockSpec((1, tn), lambda j: (0, j))],
        out_specs=pl.BlockSpec((bsz, tn), lambda j: (0, j)),
        compiler_params=_params("arbitrary"),
        name="mod",
    )(c, w_ada, b_ada.reshape(1, n))


_C_QA, _C_KVC, _C_KS, _C_KW, _C_QB, _C_KB, _C_IK, _C_IQ, _C_GN, _C_GATE = (
    0, 512, 768, 1024, 1280, 1792, 2048, 2304, 2816, 2944)
_C_END = 4992
_G_QA, _G_KS, _G_KW, _G_QB, _G_KB, _G_END = 0, 512, 640, 768, 1280, 1408


def _pos_features(pos, lane, base):
    a = (pos >> 6).astype(F32)
    b = (pos & 63).astype(F32)
    return jnp.where(lane == base, a, jnp.where(lane == base + 1, b, 0.0))


def _inproj_kernel(x_ref, mod_ref, g1_ref, w_ref, bd_ref, gain_ref,
                   qa_ref, kvc_ref, nkv_ref, gn_ref, qb_ref, dkv_ref, iq_ref, iw_ref, gate_ref):
    tm = x_ref.shape[1]
    x = x_ref[0]
    m = mod_ref[0]
    shift, scale = m[0:1], m[1:2]
    xn = x * lax.rsqrt(jnp.mean(x * x, axis=-1, keepdims=True) + RMS_EPS) * g1_ref[...]
    h = (xn * (1.0 + scale) + shift).astype(BF16)

    def seg(a, b):
        return _dot(h, w_ref[:, a:b])

    def normed(acc, goff):
        w = acc.shape[1]
        cw = min(w, 256)
        outs = []
        for c in range(0, w, cw):
            a = acc[:, c:c + cw]
            ms = _dot((a * a).astype(BF16), bd_ref[:cw, :cw])
            outs.append(a * lax.rsqrt(ms + RMS_EPS) * gain_ref[:, goff + c:goff + c + cw])
        return outs[0] if len(outs) == 1 else jnp.concatenate(outs, axis=1)

    row = lax.broadcasted_iota(I32, (tm, 128), 0)
    lane = lax.broadcasted_iota(I32, (tm, 128), 1)
    lo = lane < 64
    pos = pl.program_id(1) * tm + row
    feat_hi = _pos_features(pos, lane, 64)
    feat_lo = _pos_features(pos, lane, 0)

    qa_ref[0] = normed(seg(_C_QA, _C_QA + 512), _G_QA).astype(BF16)
    kvc_ref[0] = seg(_C_KVC, _C_KVC + 256)

    def kv_pair(ref, c0, col, goff):
        kv = seg(col, col + 256)
        kn = normed(kv[:, 0:128], goff)
        v = kv[:, 128:256]
        ref[0, :, c0:c0 + 128] = jnp.where(lo, kn, feat_hi).astype(BF16)
        ref[0, :, c0 + 128:c0 + 256] = jnp.where(lo, feat_lo, kn).astype(BF16)
        ref[0, :, c0 + 256:c0 + 384] = jnp.where(lo, v, 1.0).astype(BF16)
        ref[0, :, c0 + 384:c0 + 512] = jnp.where(lo, 1.0, v).astype(BF16)

    kv_pair(nkv_ref, 0, _C_KS, _G_KS)
    kv_pair(nkv_ref, 512, _C_KW, _G_KW)
    qb_ref[0] = normed(seg(_C_QB, _C_QB + 512), _G_QB).astype(BF16)
    kv_pair(dkv_ref, 0, _C_KB, _G_KB)

    ikw = seg(_C_IK, _C_IK + 256)
    dkv_ref[0, :, 512:640] = ikw[:, 0:128].astype(BF16)
    iw_ref[0] = ikw[:, 128:256] * (N_HEADS ** -0.5)
    iq_ref[0] = seg(_C_IQ, _C_IQ + 512).astype(BF16)
    gn_ref[0] = seg(_C_GN, _C_GN + 128)
    for c in range(0, 2048, 512):
        gate_ref[0, :, c:c + 512] = seg(_C_GATE + c, _C_GATE + c + 512).astype(BF16)


def _inproj_call(x, mod, g1, w_all, bd, gains):
    bsz, s, d = x.shape
    tm = TM_PROJ
    widths = (512, 256, 1024, 128, 512, 640, 512, 128, 2048)
    dtypes = (BF16, F32, BF16, F32, BF16, BF16, BF16, F32, BF16)
    return pl.pallas_call(
        _inproj_kernel,
        out_shape=[jax.ShapeDtypeStruct((bsz, s, w), dt) for w, dt in zip(widths, dtypes)],
        grid=(bsz, s // tm),
        in_specs=[pl.BlockSpec((1, tm, d), lambda b, i: (b, i, 0)),
                  pl.BlockSpec((1, N_ADA, d), lambda b, i: (b, 0, 0)),
                  _const_spec((1, d)),
                  _const_spec(w_all.shape),
                  _const_spec(bd.shape),
                  _const_spec(gains.shape)],
        out_specs=[pl.BlockSpec((1, tm, w), lambda b, i: (b, i, 0)) for w in widths],
        compiler_params=_params("arbitrary", "arbitrary"),
        name="inproj",
    )(x, mod, g1, w_all, bd, gains)


def _compress_kernel(t_ref, wa_ref, wb_ref, pek_ref, pev_ref, w1k_ref, w1v_ref, w2_ref,
                     gkc_ref, bd_ref, o_ref):
    n = t_ref.shape[1]
    t = t_ref[0].astype(BF16)
    first = _dot(t, wa_ref[...])
    second = _dot(t, wb_ref[...])
    second = pltpu.roll(second, n - 1, 0)
    ck = _dot(jnp.broadcast_to(pek_ref[...], (8, pek_ref.shape[1])).astype(BF16), w1k_ref[...])[0:1]
    cv = _dot(jnp.broadcast_to(pev_ref[...], (8, pev_ref.shape[1])).astype(BF16), w1v_ref[...])[0:1]
    hid = first + second + jnp.concatenate([ck, ck, cv, cv], axis=1)
    act = hid * _sigmoid(hid)
    out = _dot(act.astype(BF16), w2_ref[...])
    kc = out[:, 0:128]
    ms = _dot((kc * kc).astype(BF16), bd_ref[...])
    kcn = kc * lax.rsqrt(ms + RMS_EPS) * gkc_ref[...]
    row = lax.broadcasted_iota(I32, (n, 128), 0)
    lane = lax.broadcasted_iota(I32, (n, 128), 1)
    lo = lane < 64
    cpos = row * CMP_STRIDE + (CMP_LEN - 1)
    o_ref[0, :, 0:128] = jnp.where(lo, kcn, _pos_features(cpos, lane, 64)).astype(BF16)
    o_ref[0, :, 128:256] = jnp.where(lo, _pos_features(cpos, lane, 0), kcn).astype(BF16)
    o_ref[0, :, 256:384] = out[:, 128:256].astype(BF16)


def _compress_call(t16, wa, wb, pek, pev, w1k, w1v, w2, gkc, bd):
    bsz, n, k = t16.shape
    args = (wa, wb, pek, pev, w1k, w1v, w2, gkc, bd)
    return pl.pallas_call(
        _compress_kernel,
        out_shape=jax.ShapeDtypeStruct((bsz, n, 384), BF16),
        grid=(bsz,),
        in_specs=[pl.BlockSpec((1, n, k), lambda b: (b, 0, 0))] + [_const_spec(a.shape) for a in args],
        out_specs=pl.BlockSpec((1, n, 384), lambda b: (b, 0, 0)),
        compiler_params=_params("arbitrary"),
        name="compress",
    )(t16, *args)


def _iota2(shape):
    return lax.broadcasted_iota(I32, shape, 0), lax.broadcasted_iota(I32, shape, 1)


def _for_tiles(n, tile_fn):
    def two(p, carry):
        tile_fn(2 * p)
        tile_fn(2 * p + 1)
        return carry

    lax.fori_loop(0, n >> 1, two, 0)

    @pl.when((n & 1) == 1)
    def _():
        tile_fn(n - 1)


def _fold_max(s):
    out = s[:, 0:128]
    for c in range(128, s.shape[1], 128):
        out = jnp.maximum(out, s[:, c:c + 128])
    return out


def _finish_max(m_ref, g, last):
    mx = jnp.maximum(m_ref[g], _fold_max(last))
    m_ref[g] = jnp.broadcast_to(jnp.max(mx, axis=-1, keepdims=True), mx.shape)


def _merge_halves(lo, a, b):
    return jnp.where(lo, a, b)


def _softmax_pv(s_ref, m_ref, acc_ref, n_tiles, v_tile, lo, tq, unrolled=False):
    acc_ref[...] = jnp.zeros_like(acc_ref)

    def tile(kt):
        for g in range(2):
            s = s_ref[g, kt]
            m_rep = m_ref[g]
            p = jnp.concatenate([jnp.exp(s[:, c:c + 128] - m_rep) for c in range(0, s.shape[1], 128)], axis=1)
            acc_ref[g] += _dot(p.astype(BF16), v_tile(g, kt))

    if unrolled:
        for kt in range(n_tiles):
            tile(kt)
    else:
        _for_tiles(n_tiles, tile)
    outs = []
    for j in range(acc_ref.shape[1] // tq):
        a0 = acc_ref[0, j * tq:(j + 1) * tq, :]
        a1 = acc_ref[1, j * tq:(j + 1) * tq, :]
        num = _merge_halves(lo, a0, a1)
        den = pltpu.roll(_merge_halves(lo, a1, a0), 64, 1)
        outs.append(num / jnp.maximum(den, 1e-30))
    return outs


def _nsa_kernel(qa_ref, ckv_ref, kv_ref, gn_ref, qf_ref, ovt_ref, e3_ref, o_ref,
                qop_ref, s_ref, mx_ref, acc_ref, og_ref, selm_ref, imp_ref, keep_ref):
    tq, tk = TQ, TK
    ng = NSA_GROUPS
    hg = N_HEADS // ng
    rows = hg * tq
    n_slc = e3_ref.shape[0] * tk // SLC_LEN
    qi = pl.program_id(1)
    q0 = qi * tq
    row, lane = _iota2((tq, 128))
    lo = lane < 64
    pos = q0 + row

    qblk = qa_ref[0]
    qf = qf_ref[...]
    for j in range(hg):
        blk = qblk[:, j * 128:(j + 1) * 128]
        qop_ref[j * tq:(j + 1) * tq, :] = jnp.where(lo, blk, qf[j:j + 1, :])
        qop_ref[(hg + j) * tq:(hg + j + 1) * tq, :] = jnp.where(lo, qf[hg + j:hg + j + 1, :], blk)

    gt = _sigmoid(gn_ref[0])

    def add_gated(blocks, br, first=False):
        for j in range(hg):
            c0, c1 = 3 * j + br, 3 * (hg + j) + br
            gate = _merge_halves(lo, jnp.broadcast_to(gt[:, c0:c0 + 1], (tq, 128)),
                                 jnp.broadcast_to(gt[:, c1:c1 + 1], (tq, 128)))
            if first:
                og_ref[j * tq:(j + 1) * tq, :] = gate * blocks[j]
            else:
                og_ref[j * tq:(j + 1) * tq, :] += gate * blocks[j]

    def q_of(g):
        return qop_ref[g * rows:(g + 1) * rows, :]

    r2, c2 = _iota2((tq, tk))
    causal = jnp.where(c2 <= r2, 0.0, NEG)
    older = jnp.where(c2 > r2, 0.0, NEG)

    n_cmp = ckv_ref.shape[1]
    nrow, rcol = _iota2((n_cmp, rows))
    seen = (nrow * CMP_STRIDE + (CMP_LEN - 1)) <= q0 + (rcol & (tq - 1))
    seen_add = jnp.where(seen, 0.0, NEG)
    seen_mul = jnp.where(seen, 1.0, 0.0)
    keep_ref[...] = jnp.zeros_like(keep_ref)
    jrow, tcol = _iota2((8, tq))
    post = q0 + tcol
    cur = post >> 6
    vc_t = ckv_ref[0, :, 256:384].astype(F32).T.astype(BF16)
    cmp_t = []
    for g in range(ng):
        kc = ckv_ref[0, :, g * 128:(g + 1) * 128]
        sm = _dot_nt(kc, q_of(g)) + seen_add
        m = jnp.max(sm, axis=0, keepdims=True)
        e = jnp.exp(sm - m) * seen_mul
        p = e / jnp.maximum(jnp.sum(e, axis=0, keepdims=True), 1e-30)
        cmp_t.append(_dot(vc_t, p.astype(BF16)))

        psum = p[:, 0:tq]
        for r in range(1, hg):
            psum = psum + p[:, r * tq:(r + 1) * tq]
        p_hi = psum.astype(BF16)
        p_lo = (psum - p_hi.astype(F32)).astype(BF16)
        imp = _dot(ovt_ref[...], p_hi) + _dot(ovt_ref[...], p_lo)
        parts = []
        for a in range(n_slc // 8):
            jj = jrow + 8 * a
            forced = (jj == 0) | (jj == cur) | (jj == cur - 1)
            visible = (jj * SLC_LEN) <= post
            parts.append(jnp.where(forced, FORCE_SCORE, jnp.where(visible, imp[8 * a:8 * a + 8], -FORCE_SCORE)))
            imp_ref[8 * a:8 * a + 8, :] = parts[a]
        ranks = [jnp.zeros((8, tq), F32) for _ in parts]
        for i in range(n_slc):
            other = imp_ref[i:i + 1, :]
            for a, mine in enumerate(parts):
                if 8 * a > i:
                    ranks[a] = ranks[a] + jnp.where(other >= mine, 1.0, 0.0)
                elif 8 * a + 7 < i:
                    ranks[a] = ranks[a] + jnp.where(other > mine, 1.0, 0.0)
                else:
                    ranks[a] = (ranks[a] + jnp.where(other > mine, 1.0, 0.0)
                                + jnp.where((other == mine) & (jrow + 8 * a > i), 1.0, 0.0))
        for a, rk in enumerate(ranks):
            keep_ref[8 * a:8 * a + 8, :] = jnp.where(rk < float(SLC_TOPN), 0.0, -1.0)
        selm_ref[g] = keep_ref[...].T.astype(BF16)
    add_gated([jnp.concatenate([cmp_t[0][0:64, j * tq:(j + 1) * tq],
                                cmp_t[1][64:128, j * tq:(j + 1) * tq]], axis=0).T
               for j in range(hg)], 0, first=True)

    def slc_scores(g, kt, extra):
        k0 = pl.multiple_of(kt * tk, tk)
        kt_ = kv_ref[0, pl.ds(k0, tk), g * 128:(g + 1) * 128]
        sc = _dot_nt(q_of(g), kt_).reshape(hg, tq, tk)
        madd = _dot(selm_ref[g], e3_ref[kt])
        if extra is not None:
            madd = madd + extra
        return (sc + madd[None]).reshape(rows, tk)

    mx_ref[...] = jnp.full(mx_ref.shape, NEG, F32)

    def slc_a(kt, carry):
        for g in range(ng):
            sc = slc_scores(g, kt, None)
            s_ref[g, kt] = sc
            mx_ref[g] = jnp.maximum(mx_ref[g], _fold_max(sc))
        return carry

    _for_tiles(qi, lambda kt: slc_a(kt, 0))
    for g in range(ng):
        sc = slc_scores(g, qi, causal)
        s_ref[g, qi] = sc
        _finish_max(mx_ref, g, sc)

    def slc_v(g, kt):
        k0 = pl.multiple_of(kt * tk, tk)
        return kv_ref[0, pl.ds(k0, tk), 256 + g * 128:384 + g * 128]

    add_gated(_softmax_pv(s_ref, mx_ref, acc_ref, qi + 1, slc_v, lo, tq), 1)

    def win_k0(slot):
        return pl.multiple_of(jnp.maximum(qi - 2 + slot, 0) * tk, tk)

    for g in range(ng):
        mx_ref[g] = jnp.full((rows, 128), NEG, F32)
        for slot, base in enumerate((older, None, causal)):
            back = 2 - slot
            kt_ = kv_ref[0, pl.ds(win_k0(slot), tk), 512 + g * 128:640 + g * 128]
            sc = _dot_nt(q_of(g), kt_).reshape(hg, tq, tk)
            if back > 0:
                gone = jnp.where(qi >= back, 0.0, NEG)
                madd = gone if base is None else base + gone
            else:
                madd = base
            sc = (sc + madd).reshape(rows, tk)
            s_ref[g, slot] = sc
            if slot < 2:
                mx_ref[g] = jnp.maximum(mx_ref[g], _fold_max(sc))
            else:
                _finish_max(mx_ref, g, sc)

    def win_v(g, slot):
        return kv_ref[0, pl.ds(win_k0(slot), tk), 768 + g * 128:896 + g * 128]

    add_gated(_softmax_pv(s_ref, mx_ref, acc_ref, 3, win_v, lo, tq, unrolled=True), 2)

    for j in range(hg):
        o_ref[0, :, j * 128:(j + 1) * 128] = og_ref[j * tq:(j + 1) * tq, :].astype(BF16)


def _nsa_call(qa, ckv, nkv, gn, qf, ovt, e3):
    bsz, s, _ = qa.shape
    tq, tk = TQ, TK
    ng = NSA_GROUPS
    rows = (N_HEADS // ng) * tq
    n_cmp_pad = ckv.shape[1]
    return pl.pallas_call(
        _nsa_kernel,
        out_shape=jax.ShapeDtypeStruct((bsz, s, 512), BF16),
        grid=(bsz, s // tq),
        in_specs=[pl.BlockSpec((1, tq, 512), lambda b, i: (b, i, 0)),
                  pl.BlockSpec((1, n_cmp_pad, 384), lambda b, i: (b, 0, 0)),
                  pl.BlockSpec((1, s, 1024), lambda b, i: (b, 0, 0)),
                  pl.BlockSpec((1, tq, 128), lambda b, i: (b, i, 0)),
                  _const_spec(qf.shape), _const_spec(ovt.shape), _const_spec(e3.shape)],
        out_specs=pl.BlockSpec((1, tq, 512), lambda b, i: (b, i, 0)),
        scratch_shapes=[pltpu.VMEM((N_HEADS * tq, 128), BF16),
                        pltpu.VMEM((ng, s // tk, rows, tk), F32),
                        pltpu.VMEM((ng, rows, 128), F32),
                        pltpu.VMEM((ng, rows, 128), F32),
                        pltpu.VMEM((rows, 128), F32),
                        pltpu.VMEM((ng, tq, 128), BF16),
                        pltpu.VMEM((ovt.shape[0], tq), F32),
                        pltpu.VMEM((128, tq), F32)],
        compiler_params=_params("arbitrary", "arbitrary"),
        name="nsa",
    )(qa, ckv, nkv, gn, qf, ovt, e3)


def _dsa_kernel(qb_ref, kv_ref, iq_ref, iw_ref, qf_ref, lt_ref, o_ref,
                qop_ref, iqt_ref, key_ref, khi_ref, klo_ref, mk_ref, s_ref, mx_ref, acc_ref, cnt_ref, eqc_ref):
    tq, tk = TQ, TK
    half = N_HEADS // 2
    rows = half * tq
    qi = pl.program_id(1)
    n_kt = qi + 1
    _, lane = _iota2((tq, 128))
    lo = lane < 64
    r2, c2 = _iota2((tq, tk))
    causal = jnp.where(c2 <= r2, 0.0, NEG)

    qblk = qb_ref[0]
    qf = qf_ref[...]
    for j in range(half):
        qb_j = qblk[:, j * 128:(j + 1) * 128]
        qop_ref[j * tq:(j + 1) * tq, :] = jnp.where(lo, qb_j, qf[2 * j:2 * j + 1, :])
        qop_ref[(half + j) * tq:(half + j + 1) * tq, :] = jnp.where(lo, qf[2 * j + 1:2 * j + 2, :], qb_j)

    iqt_ref[...] = iq_ref[0].astype(F32).T.astype(BF16)
    w_t = iw_ref[0].T[0:N_HEADS]
    kidx, qidx = _iota2((tk, tq))

    def idx_tile(kt):
        k0 = pl.multiple_of(kt * tk, tk)
        ik = kv_ref[0, pl.ds(k0, tk), 512:512 + HEAD_DIM]
        tot = jnp.zeros((tk, tq), F32)
        for h in range(N_HEADS):
            sc = _dot(ik, iqt_ref[h * HEAD_DIM:(h + 1) * HEAD_DIM, :])
            tot = tot + w_t[h:h + 1, :] * jnp.maximum(sc, 0.0)
        past = jnp.where(kt < qi, tk, 0)
        tot = jnp.where(kidx <= qidx + past, tot, -jnp.inf)
        bits = pltpu.bitcast(tot, I32)
        key = bits ^ ((bits >> 31) & 0x7FFFFFFF)
        key_ref[kt] = key
        khi_ref[kt] = (key >> 16).astype(I16)
        klo_ref[kt] = ((key & 0xFFFF) - HALF16).astype(I16)

    _for_tiles(n_kt, idx_tile)

    topk = float(min(DSA_TOPK_MAX, kv_ref.shape[1] // 4))

    def count16(ref, pred_fn):
        def tile(kt, c):
            hit = jnp.where(pred_fn(ref[kt]), jnp.bfloat16(1), jnp.bfloat16(0))
            parts = [hit[r * 16:(r + 1) * 16] for r in range(tk // 16)]
            while len(parts) > 1:
                parts = [parts[i] + parts[i + 1] for i in range(0, len(parts), 2)]
            return c + parts[0].astype(F32)

        c = lax.fori_loop(0, n_kt, tile, jnp.zeros((16, tq), F32))
        return jnp.sum(c, axis=0, keepdims=True)

    def kth_half(ref, kth):
        def bit_body(i, prefix):
            bm = jnp.left_shift(jnp.int32(1), 15 - i)
            cand = ((prefix | bm) - HALF16).astype(I16)
            tot = count16(ref, lambda h: h >= cand)
            return jnp.where(tot >= kth, prefix | bm, prefix)
        return lax.fori_loop(0, 16, bit_body, jnp.zeros((1, tq), I32))

    hi_u = kth_half(khi_ref, topk)
    hi16 = (hi_u - HALF16).astype(I16)
    left = topk - count16(khi_ref, lambda h: h > hi16)

    def bucket_tile(kt):
        klo_ref[kt] = jnp.where(khi_ref[kt] == hi16, klo_ref[kt], jnp.int16(-HALF16))

    _for_tiles(n_kt, bucket_tile)
    lo_u = kth_half(klo_ref, left)
    thr = ((hi_u - HALF16) << 16) | lo_u

    cnt_ref[...] = jnp.zeros_like(cnt_ref)

    def gt_tile(kt):
        k = key_ref[kt]
        hit = jnp.where(k > thr, 1.0, 0.0)
        cnt_ref[0:8, :] += jnp.sum(hit.reshape(tk // 8, 8, tq), axis=0)
        eqc_ref[kt] = jnp.sum(jnp.where(k == thr, 1.0, 0.0).reshape(tk // 8, 8, tq), axis=0)

    _for_tiles(n_kt, gt_tile)
    need = topk - jnp.sum(cnt_ref[0:8, :], axis=0, keepdims=True)

    def prefix_tile(kt, run):
        tot = jnp.sum(eqc_ref[kt], axis=0, keepdims=True)
        eqc_ref[kt] = jnp.broadcast_to(run, (8, tq))
        return run + tot

    lax.fori_loop(0, n_kt, prefix_tile, jnp.zeros((1, tq), F32))

    def mask_tile(kt):
        k = key_ref[kt]
        eq = k == thr
        before = _dot(lt_ref[...], jnp.where(eq, 1.0, 0.0).astype(BF16)) + eqc_ref[kt][0:1, :]
        tie = jnp.where(before < need, 0.0, NEG)
        m_t = jnp.where(k > thr, 0.0, jnp.where(eq, tie, NEG))
        mk_ref[kt] = m_t.T

    _for_tiles(n_kt, mask_tile)

    def scores(g, kt, extra):
        k0 = pl.multiple_of(kt * tk, tk)
        kt_ = kv_ref[0, pl.ds(k0, tk), g * 128:(g + 1) * 128]
        sc = _dot_nt(qop_ref[g * rows:(g + 1) * rows, :], kt_).reshape(half, tq, tk)
        madd = mk_ref[kt]
        if extra is not None:
            madd = madd + extra
        return (sc + madd[None]).reshape(rows, tk)

    mx_ref[...] = jnp.full(mx_ref.shape, NEG, F32)

    def pass_a(kt, carry):
        for g in range(2):
            sc = scores(g, kt, None)
            s_ref[g, kt] = sc
            mx_ref[g] = jnp.maximum(mx_ref[g], _fold_max(sc))
        return carry

    _for_tiles(qi, lambda kt: pass_a(kt, 0))
    for g in range(2):
        sc = scores(g, qi, causal)
        s_ref[g, qi] = sc
        _finish_max(mx_ref, g, sc)

    def v_tile(g, kt):
        k0 = pl.multiple_of(kt * tk, tk)
        return kv_ref[0, pl.ds(k0, tk), 256 + g * 128:384 + g * 128]

    blocks = _softmax_pv(s_ref, mx_ref, acc_ref, n_kt, v_tile, lo, tq)
    for j in range(half):
        o_ref[0, :, j * 128:(j + 1) * 128] = blocks[j].astype(BF16)


def _dsa_call(qb, dkv, iq, iw, qf, lt):
    bsz, s, _ = qb.shape
    tq, tk = TQ, TK
    rows = (N_HEADS // 2) * tq
    return pl.pallas_call(
        _dsa_kernel,
        out_shape=jax.ShapeDtypeStruct((bsz, s, 512), BF16),
        grid=(bsz, s // tq),
        in_specs=[pl.BlockSpec((1, tq, 512), lambda b, i: (b, i, 0)),
                  pl.BlockSpec((1, s, 640), lambda b, i: (b, 0, 0)),
                  pl.BlockSpec((1, tq, 512), lambda b, i: (b, i, 0)),
                  pl.BlockSpec((1, tq, 128), lambda b, i: (b, i, 0)),
                  _const_spec(qf.shape), _const_spec(lt.shape)],
        out_specs=pl.BlockSpec((1, tq, 512), lambda b, i: (b, i, 0)),
        scratch_shapes=[pltpu.VMEM((N_HEADS * tq, 128), BF16),
                        pltpu.VMEM((N_HEADS * HEAD_DIM, tq), BF16),
                        pltpu.VMEM((s // tk, tk, tq), I32),
                        pltpu.VMEM((s // tk, tk, tq), I16),
                        pltpu.VMEM((s // tk, tk, tq), I16),
                        pltpu.VMEM((s // tk, tq, tk), F32),
                        pltpu.VMEM((2, s // tk, rows, tk), F32),
                        pltpu.VMEM((2, rows, 128), F32),
                        pltpu.VMEM((2, rows, 128), F32),
                        pltpu.VMEM((16, tq), F32),
                        pltpu.VMEM((s // tk, 8, tq), F32)],
        compiler_params=_params("arbitrary", "arbitrary"),
        name="dsa",
    )(qb, dkv, iq, iw, qf, lt)


_FF_CHUNK = 256


def _post_kernel(x_ref, oa_ref, ob_ref, gate_ref, mod_ref, g2_ref,
                 woa_ref, wob_ref, wout_ref, wg_ref, wu_ref, wd_ref, o_ref, acc_ref):
    d = x_ref.shape[2]
    m = mod_ref[0]
    gate1, shift2, scale2, gate2 = m[2:3], m[3:4], m[4:5], m[5:6]
    ga = _sigmoid(gate_ref[0, :, 0:d].astype(F32))
    gb = _sigmoid(gate_ref[0, :, d:2 * d].astype(F32))
    y = ga * _dot(oa_ref[0], woa_ref[...]) + gb * _dot(ob_ref[0], wob_ref[...])
    x1 = x_ref[0] + gate1 * _dot(y.astype(BF16), wout_ref[...])
    xn = x1 * lax.rsqrt(jnp.mean(x1 * x1, axis=-1, keepdims=True) + RMS_EPS) * g2_ref[...]
    h2 = (xn * (1.0 + scale2) + shift2).astype(BF16)
    acc_ref[...] = jnp.zeros_like(acc_ref)
    d_ff = wg_ref.shape[1]
    for c in range(0, d_ff, _FF_CHUNK):
        gg = _dot(h2, wg_ref[:, c:c + _FF_CHUNK])
        uu = _dot(h2, wu_ref[:, c:c + _FF_CHUNK])
        a = (gg * _sigmoid(gg) * uu).astype(BF16)
        acc_ref[...] += _dot(a, wd_ref[c:c + _FF_CHUNK, :])
    o_ref[0] = x1 + gate2 * acc_ref[...]


def _post_call(x, oa, ob, gates, mod, g2, woa, wob, wout, wg, wu, wd):
    bsz, s, d = x.shape
    tm = TM_PROJ
    weights = (woa, wob, wout, wg, wu, wd)
    w_specs = [pl.BlockSpec(w.shape, lambda b, i: (0, 0), pipeline_mode=pl.Buffered(1)) for w in weights]
    return pl.pallas_call(
        _post_kernel,
        out_shape=jax.ShapeDtypeStruct((bsz, s, d), F32),
        grid=(bsz, s // tm),
        in_specs=[pl.BlockSpec((1, tm, d), lambda b, i: (b, i, 0)),
                  pl.BlockSpec((1, tm, 512), lambda b, i: (b, i, 0)),
                  pl.BlockSpec((1, tm, 512), lambda b, i: (b, i, 0)),
                  pl.BlockSpec((1, tm, 2 * d), lambda b, i: (b, i, 0)),
                  pl.BlockSpec((1, N_ADA, d), lambda b, i: (b, 0, 0)),
                  _const_spec((1, d))] + w_specs,
        out_specs=pl.BlockSpec((1, tm, d), lambda b, i: (b, i, 0)),
        scratch_shapes=[pltpu.VMEM((tm, d), F32)],
        compiler_params=_params("arbitrary", "arbitrary"),
        name="post",
    )(x, oa, ob, gates, mod, g2, *weights)


def _alibi_slopes(n):
    return np.exp2(-8.0 * np.arange(1, n + 1, dtype=np.float64) / n)


def _query_features(low_heads):
    sl = _alibi_slopes(N_HEADS)
    qf = np.zeros((N_HEADS, 128), np.float32)
    for h in range(N_HEADS):
        base = 64 if h in low_heads else 0
        qf[h, base] = 64.0 * sl[h]
        qf[h, base + 1] = sl[h]
    return jnp.asarray(qf, BF16)


def _block_diag_mean(n):
    i = np.arange(n)
    return jnp.asarray((i[:, None] // HEAD_DIM == i[None, :] // HEAD_DIM) / HEAD_DIM, BF16)


def _pad_cols(w, n):
    return jnp.pad(w, ((0, 0), (0, n - w.shape[1])))


def _layer(x, c, w_ada, b_ada, g_norm1, g_norm2, w_in, g_q_a, g_kc_a, g_ks_a, g_kw_a,
           pe_ck, pe_cv, w_ck1, w_ck2, w_cv1, w_cv2, g_q_b, g_k_b, w_o_a, w_o_b, w_out,
           w_ff_gate, w_ff_up, w_ff_down):
    bsz, s, d = x.shape
    hd = HEAD_DIM
    n_slc = s // SLC_LEN

    perm = np.array([(j + 4 * half) * hd + dd for j in range(4) for half in range(2) for dd in range(hd)])

    sizes = (512, 128, 128, 128, 128, 128, 128, 24, 512, 64, 64, 512, 64, 8, d, d)
    offs = np.concatenate([[0], np.cumsum(sizes)])
    col = lambda i: w_in[:, offs[i]:offs[i + 1]]
    dup = lambda w: jnp.concatenate([w, w], axis=1)
    w_all = jnp.concatenate([
        col(0)[:, perm], col(1), col(2), col(3), col(4), col(5), col(6),
        col(8), dup(col(9)), dup(col(10)), dup(col(12)), _pad_cols(col(13), 128),
        col(11) * (hd ** -0.5), _pad_cols(col(7), 128), col(14), col(15)], axis=1).astype(BF16)
    assert w_all.shape[1] == _C_END
    qscale = hd ** -0.5
    gains = jnp.concatenate([jnp.tile(g_q_a, 8) * qscale, jnp.tile(g_ks_a, 2), jnp.tile(g_kw_a, 2),
                             jnp.tile(g_q_b, 8) * qscale, jnp.tile(g_k_b, 2)]).reshape(1, _G_END)
    bd = _block_diag_mean(256)

    mod = _mod_call(c, w_ada, b_ada).reshape(bsz, N_ADA, d)
    qa, kvc, nkv, gn, qb, dkv, iq, iw, gates = _inproj_call(
        x, mod, g_norm1.reshape(1, d), w_all, bd, gains)

    half_len = CMP_LEN // 2

    def cmp_weight(w1k, w1v, first):
        wk = w1k.reshape(CMP_LEN, hd, CMP_HIDDEN)
        wv = w1v.reshape(CMP_LEN, hd, CMP_HIDDEN)
        sl = slice(0, half_len) if first else slice(half_len, CMP_LEN)
        big = jnp.zeros((half_len, 4 * hd, 4 * CMP_HIDDEN), F32)
        for j, wsrc in enumerate((wk, wk, wv, wv)):
            big = big.at[:, j * hd:(j + 1) * hd, j * CMP_HIDDEN:(j + 1) * CMP_HIDDEN].set(wsrc[sl])
        return big.reshape(half_len * 4 * hd, 4 * CMP_HIDDEN).astype(BF16)

    w2 = jnp.zeros((4 * CMP_HIDDEN, 4 * hd), F32)
    for j, wsrc in enumerate((w_ck2, w_ck2, w_cv2, w_cv2)):
        w2 = w2.at[j * CMP_HIDDEN:(j + 1) * CMP_HIDDEN, j * hd:(j + 1) * hd].set(wsrc)
    ckv = _compress_call(
        kvc.reshape(bsz, s // CMP_STRIDE, CMP_STRIDE * 4 * hd),
        cmp_weight(w_ck1, w_cv1, True), cmp_weight(w_ck1, w_cv1, False),
        pe_ck.reshape(1, CMP_LEN * hd), pe_cv.reshape(1, CMP_LEN * hd),
        w_ck1.astype(BF16), w_cv1.astype(BF16), w2.astype(BF16),
        jnp.tile(g_kc_a, 2).reshape(1, 128), _block_diag_mean(128))

    n_cmp_pad = s // CMP_STRIDE
    cs = np.arange(n_cmp_pad)[:, None] * CMP_STRIDE
    jj = np.arange(128)[None, :]
    ov = ((cs < (jj + 1) * SLC_LEN) & (cs + CMP_LEN > jj * SLC_LEN) & (jj < n_slc)
          & (np.arange(n_cmp_pad)[:, None] < (s - CMP_LEN) // CMP_STRIDE + 1))
    kk = np.arange(s)
    e3 = (np.arange(128)[:, None] == (kk // SLC_LEN)[None, :]) * MASK_BIG
    e3 = e3.reshape(128, s // TK, TK).transpose(1, 0, 2)
    o_a = _nsa_call(qa, ckv, nkv, gn, _query_features(range(4)),
                    jnp.asarray(ov.T[:n_slc], BF16), jnp.asarray(e3, BF16))

    lt = jnp.asarray(np.arange(TK)[None, :] < np.arange(TK)[:, None], BF16)
    o_b = _dsa_call(qb, dkv, iq, iw, _query_features(range(0, N_HEADS, 2)), lt)

    return _post_call(x, o_a, o_b, gates, mod, g_norm2.reshape(1, d),
                      w_o_a[perm].astype(BF16), w_o_b.astype(BF16), w_out.astype(BF16),
                      w_ff_gate.astype(BF16), w_ff_up.astype(BF16), w_ff_down.astype(BF16))


def kernel(x, c, w_ada, b_ada, g_norm1, g_norm2, w_in, g_q_a, g_kc_a, g_ks_a, g_kw_a, pe_ck, pe_cv, w_ck1, w_ck2, w_cv1, w_cv2, g_q_b, g_k_b, w_o_a, w_o_b, w_out, w_ff_gate, w_ff_up, w_ff_down):
    for l in range(w_ada.shape[0]):
        x = _layer(x, c, w_ada[l], b_ada[l], g_norm1[l], g_norm2[l], w_in[l],
                   g_q_a[l], g_kc_a[l], g_ks_a[l], g_kw_a[l], pe_ck[l], pe_cv[l],
                   w_ck1[l], w_ck2[l], w_cv1[l], w_cv2[l], g_q_b[l], g_k_b[l],
                   w_o_a[l], w_o_b[l], w_out[l], w_ff_gate[l], w_ff_up[l], w_ff_down[l])
    return x
```

```python
import numpy as np
import jax
import jax.numpy as jnp
from jax import lax
from jax.experimental import pallas as pl
from jax.experimental.pallas import tpu as pltpu

F32 = jnp.float32
BF16 = jnp.bfloat16
I32 = jnp.int32
I16 = jnp.int16

HEAD_DIM = 64
N_HEADS = 8
NSA_GROUPS = 2
CMP_LEN = 32
CMP_STRIDE = 16
CMP_HIDDEN = 128
SLC_LEN = 64
SLC_TOPN = 16
WINDOW = 512
DSA_TOPK_MAX = 256
N_ADA = 6
RMS_EPS = 1e-6
FORCE_SCORE = 1e9
NEG = -1e30
MASK_BIG = 2.0 ** 100
HALF16 = 1 << 15
BLOCK_LANE_HI = 72
BLOCK_LANE_LO = 8

TQ = 256
TK = 256
TM_PROJ = 512
VMEM_LIMIT = 56 * 1024 * 1024


def _dot(a, b):
    return jnp.dot(a, b, preferred_element_type=F32)


def _dot_nt(a, b):
    return lax.dot_general(a, b, (((1,), (1,)), ((), ())), preferred_element_type=F32)


def _sigmoid(x):
    return 1.0 / (1.0 + jnp.exp(-x))


def _params(*sem):
    return pltpu.CompilerParams(dimension_semantics=sem, vmem_limit_bytes=VMEM_LIMIT)


def _const_spec(shape):
    nd = len(shape)
    return pl.BlockSpec(shape, lambda *_: (0,) * nd)


def _mod_kernel(c_ref, w_ref, b_ref, o_ref):
    c = c_ref[...]
    a = c * _sigmoid(c)
    o_ref[...] = jnp.dot(a, w_ref[...], preferred_element_type=F32,
                         precision=lax.Precision.HIGHEST) + b_ref[...]


def _mod_call(c, w_ada, b_ada):
    bsz, d = c.shape
    n = w_ada.shape[1]
    tn = 512
    return pl.pallas_call(
        _mod_kernel,
        out_shape=jax.ShapeDtypeStruct((bsz, n), F32),
        grid=(n // tn,),
        in_specs=[pl.BlockSpec((bsz, d), lambda j: (0, 0)),
                  pl.BlockSpec((d, tn), lambda j: (0, j)),
                  pl.BlockSpec((1, tn), lambda j: (0, j))],
        out_specs=pl.BlockSpec((bsz, tn), lambda j: (0, j)),
        compiler_params=_params("arbitrary"),
        name="mod",
    )(c, w_ada, b_ada.reshape(1, n))


_C_QA, _C_KVC, _C_KS, _C_KW, _C_QB, _C_KB, _C_IK, _C_IQ, _C_GN, _C_GATE = (
    0, 512, 768, 1024, 1280, 1792, 2048, 2304, 2816, 2944)
_C_END = 4992
_G_QA, _G_KS, _G_KW, _G_QB, _G_KB, _G_END = 0, 512, 640, 768, 1280, 1408


def _pos_features(pos, lane, base):
    a = (pos >> 6).astype(F32)
    b = (pos & 63).astype(F32)
    return jnp.where(lane == base, a, jnp.where(lane == base + 1, b, 0.0))


def _inproj_kernel(x_ref, mod_ref, g1_ref, w_ref, bd_ref, gain_ref,
                   qa_ref, kvc_ref, nkv_ref, gn_ref, qb_ref, dkv_ref, iq_ref, iw_ref, gate_ref):
    tm = x_ref.shape[1]
    x = x_ref[0]
    m = mod_ref[0]
    shift, scale = m[0:1], m[1:2]
    xn = x * lax.rsqrt(jnp.mean(x * x, axis=-1, keepdims=True) + RMS_EPS) * g1_ref[...]
    h = (xn * (1.0 + scale) + shift).astype(BF16)

    def seg(a, b):
        return _dot(h, w_ref[:, a:b])

    def normed(acc, goff):
        w = acc.shape[1]
        cw = min(w, 256)
        outs = []
        for c in range(0, w, cw):
            a = acc[:, c:c + cw]
            ms = _dot((a * a).astype(BF16), bd_ref[:cw, :cw])
            outs.append(a * lax.rsqrt(ms + RMS_EPS) * gain_ref[:, goff + c:goff + c + cw])
        return outs[0] if len(outs) == 1 else jnp.concatenate(outs, axis=1)

    row = lax.broadcasted_iota(I32, (tm, 128), 0)
    lane = lax.broadcasted_iota(I32, (tm, 128), 1)
    lo = lane < 64
    pos = pl.program_id(1) * tm + row
    feat_hi = _pos_features(pos, lane, 64)
    feat_lo = _pos_features(pos, lane, 0)

    qa_ref[0] = normed(seg(_C_QA, _C_QA + 512), _G_QA).astype(BF16)
    kvc = seg(_C_KVC, _C_KVC + 256)
    kvc_ref[0, 0] = kvc[:, 0:128]
    kvc_ref[0, 1] = kvc[:, 128:256]

    def kv_pair(ref, c0, col, goff, block_mark=False):
        kv = seg(col, col + 256)
        kn = normed(kv[:, 0:128], goff)
        v = kv[:, 128:256]
        f_hi, f_lo = feat_hi, feat_lo
        if block_mark:
            blk = pos >> 6
            f_hi = f_hi + jnp.where(lane == BLOCK_LANE_HI + blk, MASK_BIG, 0.0)
            f_lo = f_lo + jnp.where(lane == BLOCK_LANE_LO + blk, MASK_BIG, 0.0)
        ref[0, :, c0:c0 + 128] = jnp.where(lo, kn, f_hi).astype(BF16)
        ref[0, :, c0 + 128:c0 + 256] = jnp.where(lo, f_lo, kn).astype(BF16)
        ref[0, :, c0 + 256:c0 + 384] = jnp.where(lo, v, 1.0).astype(BF16)
        ref[0, :, c0 + 384:c0 + 512] = jnp.where(lo, 1.0, v).astype(BF16)

    kv_pair(nkv_ref, 0, _C_KS, _G_KS, block_mark=True)
    kv_pair(nkv_ref, 512, _C_KW, _G_KW)
    qb_ref[0] = normed(seg(_C_QB, _C_QB + 512), _G_QB).astype(BF16)
    kv_pair(dkv_ref, 0, _C_KB, _G_KB)

    ikw = seg(_C_IK, _C_IK + 256)
    dkv_ref[0, :, 512:640] = ikw[:, 0:128].astype(BF16)
    iw_ref[0] = ikw[:, 128:256] * (N_HEADS ** -0.5)
    iq_ref[0] = seg(_C_IQ, _C_IQ + 512).astype(BF16)
    gn_ref[0] = seg(_C_GN, _C_GN + 128)
    for c in range(0, 2048, 512):
        gate_ref[0, :, c:c + 512] = seg(_C_GATE + c, _C_GATE + c + 512).astype(BF16)


def _inproj_call(x, mod, g1, w_all, bd, gains):
    bsz, s, d = x.shape
    tm = TM_PROJ
    widths = (512, 256, 1024, 128, 512, 640, 512, 128, 2048)
    dtypes = (BF16, F32, BF16, F32, BF16, BF16, BF16, F32, BF16)
    out_shape = [jax.ShapeDtypeStruct((bsz, s, w), dt) for w, dt in zip(widths, dtypes)]
    out_specs = [pl.BlockSpec((1, tm, w), lambda b, i: (b, i, 0)) for w in widths]
    out_shape[1] = jax.ShapeDtypeStruct((bsz, 2, s, 128), F32)
    out_specs[1] = pl.BlockSpec((1, 2, tm, 128), lambda b, i: (b, 0, i, 0))
    return pl.pallas_call(
        _inproj_kernel,
        out_shape=out_shape,
        grid=(bsz, s // tm),
        in_specs=[pl.BlockSpec((1, tm, d), lambda b, i: (b, i, 0)),
                  pl.BlockSpec((1, N_ADA, d), lambda b, i: (b, 0, 0)),
                  _const_spec((1, d)),
                  _const_spec(w_all.shape),
                  _const_spec(bd.shape),
                  _const_spec(gains.shape)],
        out_specs=out_specs,
        compiler_params=_params("arbitrary", "arbitrary"),
        name="inproj",
    )(x, mod, g1, w_all, bd, gains)


def _compress_kernel(t_ref, wa_ref, wb_ref, pek_ref, pev_ref, w1k_ref, w1v_ref, w2_ref,
                     gkc_ref, bd_ref, o_ref):
    n = t_ref.shape[2] // CMP_STRIDE
    first = jnp.zeros((n, wa_ref.shape[2]), F32)
    second = jnp.zeros((n, wb_ref.shape[2]), F32)
    for l in range(CMP_STRIDE):
        rows_l = pl.ds(l, n, stride=CMP_STRIDE)
        tok = jnp.concatenate([t_ref[0, 0, rows_l, :], t_ref[0, 1, rows_l, :]], axis=1).astype(BF16)
        first = first + _dot(tok, wa_ref[l])
        second = second + _dot(tok, wb_ref[l])
    second = pltpu.roll(second, n - 1, 0)
    ck = _dot(jnp.broadcast_to(pek_ref[...], (8, pek_ref.shape[1])).astype(BF16), w1k_ref[...])[0:1]
    cv = _dot(jnp.broadcast_to(pev_ref[...], (8, pev_ref.shape[1])).astype(BF16), w1v_ref[...])[0:1]
    hid = first + second + jnp.concatenate([ck, ck, cv, cv], axis=1)
    act = hid * _sigmoid(hid)
    out = _dot(act.astype(BF16), w2_ref[...])
    kc = out[:, 0:128]
    ms = _dot((kc * kc).astype(BF16), bd_ref[...])
    kcn = kc * lax.rsqrt(ms + RMS_EPS) * gkc_ref[...]
    row = lax.broadcasted_iota(I32, (n, 128), 0)
    lane = lax.broadcasted_iota(I32, (n, 128), 1)
    lo = lane < 64
    cpos = row * CMP_STRIDE + (CMP_LEN - 1)
    o_ref[0, :, 0:128] = jnp.where(lo, kcn, _pos_features(cpos, lane, 64)).astype(BF16)
    o_ref[0, :, 128:256] = jnp.where(lo, _pos_features(cpos, lane, 0), kcn).astype(BF16)
    o_ref[0, :, 256:384] = out[:, 128:256].astype(BF16)


def _compress_call(kvc, wa, wb, pek, pev, w1k, w1v, w2, gkc, bd):
    bsz, _, s, k = kvc.shape
    n = s // CMP_STRIDE
    args = (wa, wb, pek, pev, w1k, w1v, w2, gkc, bd)
    return pl.pallas_call(
        _compress_kernel,
        out_shape=jax.ShapeDtypeStruct((bsz, n, 384), BF16),
        grid=(bsz,),
        in_specs=[pl.BlockSpec((1, 2, s, k), lambda b: (b, 0, 0, 0))] + [_const_spec(a.shape) for a in args],
        out_specs=pl.BlockSpec((1, n, 384), lambda b: (b, 0, 0)),
        compiler_params=_params("arbitrary"),
        name="compress",
    )(kvc, *args)


def _iota2(shape):
    return lax.broadcasted_iota(I32, shape, 0), lax.broadcasted_iota(I32, shape, 1)


def _for_tiles(n, tile_fn):
    def two(p, carry):
        tile_fn(2 * p)
        tile_fn(2 * p + 1)
        return carry

    lax.fori_loop(0, n >> 1, two, 0)

    @pl.when((n & 1) == 1)
    def _():
        tile_fn(n - 1)


def _fold_max(s):
    out = s[:, 0:128]
    for c in range(128, s.shape[1], 128):
        out = jnp.maximum(out, s[:, c:c + 128])
    return out


def _finish_max(m_ref, g, last):
    mx = jnp.maximum(m_ref[g], _fold_max(last))
    m_ref[g] = jnp.broadcast_to(jnp.max(mx, axis=-1, keepdims=True), mx.shape)


def _merge_halves(lo, a, b):
    return jnp.where(lo, a, b)


def _softmax_pv(s_ref, m_ref, acc_ref, n_tiles, v_tile, lo, tq, unrolled=False):
    acc_ref[...] = jnp.zeros_like(acc_ref)

    def tile(kt):
        for g in range(2):
            s = s_ref[g, kt]
            m_rep = m_ref[g]
            p = jnp.concatenate([jnp.exp(s[:, c:c + 128] - m_rep) for c in range(0, s.shape[1], 128)], axis=1)
            acc_ref[g] += _dot(p.astype(BF16), v_tile(g, kt))

    if unrolled:
        for kt in range(n_tiles):
            tile(kt)
    else:
        _for_tiles(n_tiles, tile)
    outs = []
    for j in range(acc_ref.shape[1] // tq):
        a0 = acc_ref[0, j * tq:(j + 1) * tq, :]
        a1 = acc_ref[1, j * tq:(j + 1) * tq, :]
        num = _merge_halves(lo, a0, a1)
        den = pltpu.roll(_merge_halves(lo, a1, a0), 64, 1)
        outs.append(num / jnp.maximum(den, 1e-30))
    return outs


def _nsa_kernel(qa_ref, ckv_ref, kv_ref, gn_ref, qf_ref, ovt_ref, o_ref,
                qop_ref, s_ref, mx_ref, acc_ref, og_ref, imp_ref, keep_ref):
    tq, tk = TQ, TK
    ng = NSA_GROUPS
    hg = N_HEADS // ng
    rows = hg * tq
    n_slc = ovt_ref.shape[0]
    qi = pl.program_id(1)
    q0 = qi * tq
    row, lane = _iota2((tq, 128))
    lo = lane < 64
    pos = q0 + row

    qblk = qa_ref[0]
    qf = qf_ref[...]
    for j in range(hg):
        blk = qblk[:, j * 128:(j + 1) * 128]
        qop_ref[j * tq:(j + 1) * tq, :] = jnp.where(lo, blk, qf[j:j + 1, :])
        qop_ref[(hg + j) * tq:(hg + j + 1) * tq, :] = jnp.where(lo, qf[hg + j:hg + j + 1, :], blk)

    gt = _sigmoid(gn_ref[0])

    def add_gated(blocks, br, first=False):
        for j in range(hg):
            c0, c1 = 3 * j + br, 3 * (hg + j) + br
            gate = _merge_halves(lo, jnp.broadcast_to(gt[:, c0:c0 + 1], (tq, 128)),
                                 jnp.broadcast_to(gt[:, c1:c1 + 1], (tq, 128)))
            if first:
                og_ref[j * tq:(j + 1) * tq, :] = gate * blocks[j]
            else:
                og_ref[j * tq:(j + 1) * tq, :] += gate * blocks[j]

    def q_of(g):
        return qop_ref[g * rows:(g + 1) * rows, :]

    r2, c2 = _iota2((tq, tk))
    causal = jnp.where(c2 <= r2, 0.0, NEG)
    older = jnp.where(c2 > r2, 0.0, NEG)

    n_cmp = ckv_ref.shape[1]
    nrow, rcol = _iota2((n_cmp, rows))
    seen = (nrow * CMP_STRIDE + (CMP_LEN - 1)) <= q0 + (rcol & (tq - 1))
    seen_add = jnp.where(seen, 0.0, NEG)
    seen_mul = jnp.where(seen, 1.0, 0.0)
    jrow, tcol = _iota2((8, tq))
    post = q0 + tcol
    cur = post >> 6
    vc_t = ckv_ref[0, :, 256:384].astype(F32).T.astype(BF16)
    cmp_t = []
    for g in range(ng):
        kc = ckv_ref[0, :, g * 128:(g + 1) * 128]
        sm = _dot_nt(kc, q_of(g)) + seen_add
        m = jnp.max(sm, axis=0, keepdims=True)
        e = jnp.exp(sm - m) * seen_mul
        p = e / jnp.maximum(jnp.sum(e, axis=0, keepdims=True), 1e-30)
        cmp_t.append(_dot(vc_t, p.astype(BF16)))

        psum = p[:, 0:tq]
        for r in range(1, hg):
            psum = psum + p[:, r * tq:(r + 1) * tq]
        p_hi = psum.astype(BF16)
        p_lo = (psum - p_hi.astype(F32)).astype(BF16)
        imp = _dot(ovt_ref[...], p_hi) + _dot(ovt_ref[...], p_lo)
        parts = []
        for a in range(n_slc // 8):
            jj = jrow + 8 * a
            forced = (jj == 0) | (jj == cur) | (jj == cur - 1)
            visible = (jj * SLC_LEN) <= post
            parts.append(jnp.where(forced, FORCE_SCORE, jnp.where(visible, imp[8 * a:8 * a + 8], -FORCE_SCORE)))
            imp_ref[8 * a:8 * a + 8, :] = parts[a]
        ranks = [jnp.zeros((8, tq), F32) for _ in parts]
        for i in range(n_slc):
            other = imp_ref[i:i + 1, :]
            for a, mine in enumerate(parts):
                if 8 * a > i:
                    ranks[a] = ranks[a] + jnp.where(other >= mine, 1.0, 0.0)
                elif 8 * a + 7 < i:
                    ranks[a] = ranks[a] + jnp.where(other > mine, 1.0, 0.0)
                else:
                    ranks[a] = (ranks[a] + jnp.where(other > mine, 1.0, 0.0)
                                + jnp.where((other == mine) & (jrow + 8 * a > i), 1.0, 0.0))
        base = BLOCK_LANE_LO if g else BLOCK_LANE_HI
        keep_ref[...] = jnp.zeros_like(keep_ref)
        for a, rk in enumerate(ranks):
            keep_ref[base + 8 * a:base + 8 * a + 8, :] = jnp.where(rk < float(SLC_TOPN), 0.0, -1.0)
        flags = keep_ref[...].T
        for j in range(hg):
            h = g * hg + j
            feat = (qf[h:h + 1, :].astype(F32) + flags).astype(BF16)
            blk = qblk[:, j * 128:(j + 1) * 128]
            qop_ref[h * tq:(h + 1) * tq, :] = jnp.where(lo, feat, blk) if g else jnp.where(lo, blk, feat)
    add_gated([jnp.concatenate([cmp_t[0][0:64, j * tq:(j + 1) * tq],
                                cmp_t[1][64:128, j * tq:(j + 1) * tq]], axis=0).T
               for j in range(hg)], 0, first=True)

    def slc_scores(g, kt, extra):
        k0 = pl.multiple_of(kt * tk, tk)
        kt_ = kv_ref[0, pl.ds(k0, tk), g * 128:(g + 1) * 128]
        sc = _dot_nt(q_of(g), kt_)
        if extra is not None:
            sc = (sc.reshape(hg, tq, tk) + extra[None]).reshape(rows, tk)
        return sc

    mx_ref[...] = jnp.full(mx_ref.shape, NEG, F32)

    def slc_a(kt, carry):
        for g in range(ng):
            sc = slc_scores(g, kt, None)
            s_ref[g, kt] = sc
            mx_ref[g] = jnp.maximum(mx_ref[g], _fold_max(sc))
        return carry

    _for_tiles(qi, lambda kt: slc_a(kt, 0))
    for g in range(ng):
        sc = slc_scores(g, qi, causal)
        s_ref[g, qi] = sc
        _finish_max(mx_ref, g, sc)

    def slc_v(g, kt):
        k0 = pl.multiple_of(kt * tk, tk)
        return kv_ref[0, pl.ds(k0, tk), 256 + g * 128:384 + g * 128]

    add_gated(_softmax_pv(s_ref, mx_ref, acc_ref, qi + 1, slc_v, lo, tq), 1)

    def win_k0(slot):
        return pl.multiple_of(jnp.maximum(qi - 2 + slot, 0) * tk, tk)

    for g in range(ng):
        mx_ref[g] = jnp.full((rows, 128), NEG, F32)
        for slot, base in enumerate((older, None, causal)):
            back = 2 - slot
            kt_ = kv_ref[0, pl.ds(win_k0(slot), tk), 512 + g * 128:640 + g * 128]
            sc = _dot_nt(q_of(g), kt_).reshape(hg, tq, tk)
            if back > 0:
                gone = jnp.where(qi >= back, 0.0, NEG)
                madd = gone if base is None else base + gone
            else:
                madd = base
            sc = (sc + madd).reshape(rows, tk)
            s_ref[g, slot] = sc
            if slot < 2:
                mx_ref[g] = jnp.maximum(mx_ref[g], _fold_max(sc))
            else:
                _finish_max(mx_ref, g, sc)

    def win_v(g, slot):
        return kv_ref[0, pl.ds(win_k0(slot), tk), 768 + g * 128:896 + g * 128]

    add_gated(_softmax_pv(s_ref, mx_ref, acc_ref, 3, win_v, lo, tq, unrolled=True), 2)

    for j in range(hg):
        o_ref[0, :, j * 128:(j + 1) * 128] = og_ref[j * tq:(j + 1) * tq, :].astype(BF16)


def _nsa_call(qa, ckv, nkv, gn, qf, ovt):
    bsz, s, _ = qa.shape
    tq, tk = TQ, TK
    ng = NSA_GROUPS
    rows = (N_HEADS // ng) * tq
    n_cmp_pad = ckv.shape[1]
    assert BLOCK_LANE_LO + ovt.shape[0] <= 64
    return pl.pallas_call(
        _nsa_kernel,
        out_shape=jax.ShapeDtypeStruct((bsz, s, 512), BF16),
        grid=(bsz, s // tq),
        in_specs=[pl.BlockSpec((1, tq, 512), lambda b, i: (b, i, 0)),
                  pl.BlockSpec((1, n_cmp_pad, 384), lambda b, i: (b, 0, 0)),
                  pl.BlockSpec((1, s, 1024), lambda b, i: (b, 0, 0)),
                  pl.BlockSpec((1, tq, 128), lambda b, i: (b, i, 0)),
                  _const_spec(qf.shape), _const_spec(ovt.shape)],
        out_specs=pl.BlockSpec((1, tq, 512), lambda b, i: (b, i, 0)),
        scratch_shapes=[pltpu.VMEM((N_HEADS * tq, 128), BF16),
                        pltpu.VMEM((ng, s // tk, rows, tk), F32),
                        pltpu.VMEM((ng, rows, 128), F32),
                        pltpu.VMEM((ng, rows, 128), F32),
                        pltpu.VMEM((rows, 128), F32),
                        pltpu.VMEM((ovt.shape[0], tq), F32),
                        pltpu.VMEM((128, tq), F32)],
        compiler_params=_params("arbitrary", "arbitrary"),
        name="nsa",
    )(qa, ckv, nkv, gn, qf, ovt)


def _dsa_kernel(qb_ref, kv_ref, iq_ref, iw_ref, qf_ref, lt_ref, o_ref,
                qop_ref, iqt_ref, key_ref, khi_ref, klo_ref, mk_ref, s_ref, mx_ref, acc_ref, cnt_ref, eqc_ref):
    tq, tk = TQ, TK
    half = N_HEADS // 2
    rows = half * tq
    qi = pl.program_id(1)
    n_kt = qi + 1
    _, lane = _iota2((tq, 128))
    lo = lane < 64

    qblk = qb_ref[0]
    qf = qf_ref[...]
    for j in range(half):
        qb_j = qblk[:, j * 128:(j + 1) * 128]
        qop_ref[j * tq:(j + 1) * tq, :] = jnp.where(lo, qb_j, qf[2 * j:2 * j + 1, :])
        qop_ref[(half + j) * tq:(half + j + 1) * tq, :] = jnp.where(lo, qf[2 * j + 1:2 * j + 2, :], qb_j)

    topk = min(DSA_TOPK_MAX, kv_ref.shape[1] // 4)

    @pl.when((qi + 1) * tq > topk)
    def _():
        _dsa_select(qi, float(topk), kv_ref, iq_ref, iw_ref, lt_ref,
                    iqt_ref, key_ref, khi_ref, klo_ref, mk_ref, cnt_ref, eqc_ref)

    @pl.when((qi + 1) * tq <= topk)
    def _():
        def clear(kt):
            mk_ref[kt] = jnp.zeros((tq, tk), F32)
        _for_tiles(n_kt, clear)

    _dsa_attend(qi, lo, kv_ref, qop_ref, mk_ref, s_ref, mx_ref, acc_ref, o_ref)


def _dsa_select(qi, topk, kv_ref, iq_ref, iw_ref, lt_ref, iqt_ref, key_ref, khi_ref, klo_ref, mk_ref,
                cnt_ref, eqc_ref):
    tq, tk = TQ, TK
    n_kt = qi + 1

    iqt_ref[...] = iq_ref[0].astype(F32).T.astype(BF16)
    w_t = iw_ref[0].T[0:N_HEADS]
    kidx, qidx = _iota2((tk, tq))

    def idx_tile(kt):
        k0 = pl.multiple_of(kt * tk, tk)
        ik = kv_ref[0, pl.ds(k0, tk), 512:512 + HEAD_DIM]
        tot = jnp.zeros((tk, tq), F32)
        for h in range(N_HEADS):
            sc = _dot(ik, iqt_ref[h * HEAD_DIM:(h + 1) * HEAD_DIM, :])
            tot = tot + w_t[h:h + 1, :] * jnp.maximum(sc, 0.0)
        past = jnp.where(kt < qi, tk, 0)
        tot = jnp.where(kidx <= qidx + past, tot, -jnp.inf)
        bits = pltpu.bitcast(tot, I32)
        key = bits ^ ((bits >> 31) & 0x7FFFFFFF)
        key_ref[kt] = key
        khi_ref[kt] = (key >> 16).astype(I16)
        klo_ref[kt] = ((key & 0xFFFF) - HALF16).astype(I16)

    _for_tiles(n_kt, idx_tile)


    def count16(ref, pred_fn):
        def tile(kt, c):
            hit = jnp.where(pred_fn(ref[kt]), jnp.bfloat16(1), jnp.bfloat16(0))
            parts = [hit[r * 16:(r + 1) * 16] for r in range(tk // 16)]
            while len(parts) > 1:
                parts = [parts[i] + parts[i + 1] for i in range(0, len(parts), 2)]
            return c + parts[0].astype(F32)

        c = lax.fori_loop(0, n_kt, tile, jnp.zeros((16, tq), F32))
        return jnp.sum(c, axis=0, keepdims=True)

    def kth_half(ref, kth):
        def bit_body(i, prefix):
            bm = jnp.left_shift(jnp.int32(1), 15 - i)
            cand = ((prefix | bm) - HALF16).astype(I16)
            tot = count16(ref, lambda h: h >= cand)
            return jnp.where(tot >= kth, prefix | bm, prefix)
        return lax.fori_loop(0, 16, bit_body, jnp.zeros((1, tq), I32))

    hi_u = kth_half(khi_ref, topk)
    hi16 = (hi_u - HALF16).astype(I16)
    left = topk - count16(khi_ref, lambda h: h > hi16)

    def bucket_tile(kt):
        klo_ref[kt] = jnp.where(khi_ref[kt] == hi16, klo_ref[kt], jnp.int16(-HALF16))

    _for_tiles(n_kt, bucket_tile)
    lo_u = kth_half(klo_ref, left)
    thr = ((hi_u - HALF16) << 16) | lo_u

    cnt_ref[...] = jnp.zeros_like(cnt_ref)

    def gt_tile(kt):
        k = key_ref[kt]
        hit = jnp.where(k > thr, 1.0, 0.0)
        cnt_ref[0:8, :] += jnp.sum(hit.reshape(tk // 8, 8, tq), axis=0)
        eqc_ref[kt] = jnp.sum(jnp.where(k == thr, 1.0, 0.0).reshape(tk // 8, 8, tq), axis=0)

    _for_tiles(n_kt, gt_tile)
    need = topk - jnp.sum(cnt_ref[0:8, :], axis=0, keepdims=True)

    def prefix_tile(kt, run):
        tot = jnp.sum(eqc_ref[kt], axis=0, keepdims=True)
        eqc_ref[kt] = jnp.broadcast_to(run, (8, tq))
        return run + tot

    lax.fori_loop(0, n_kt, prefix_tile, jnp.zeros((1, tq), F32))

    def mask_tile(kt):
        k = key_ref[kt]
        eq = k == thr
        before = _dot(lt_ref[...], jnp.where(eq, 1.0, 0.0).astype(BF16)) + eqc_ref[kt][0:1, :]
        tie = jnp.where(before < need, 0.0, NEG)
        m_t = jnp.where(k > thr, 0.0, jnp.where(eq, tie, NEG))
        mk_ref[kt] = m_t.T

    _for_tiles(n_kt, mask_tile)


def _dsa_attend(qi, lo, kv_ref, qop_ref, mk_ref, s_ref, mx_ref, acc_ref, o_ref):
    tq, tk = TQ, TK
    half = N_HEADS // 2
    rows = half * tq
    n_kt = qi + 1
    r2, c2 = _iota2((tq, tk))
    causal = jnp.where(c2 <= r2, 0.0, NEG)

    def scores(g, kt, extra):
        k0 = pl.multiple_of(kt * tk, tk)
        kt_ = kv_ref[0, pl.ds(k0, tk), g * 128:(g + 1) * 128]
        sc = _dot_nt(qop_ref[g * rows:(g + 1) * rows, :], kt_).reshape(half, tq, tk)
        madd = mk_ref[kt]
        if extra is not None:
            madd = madd + extra
        return (sc + madd[None]).reshape(rows, tk)

    mx_ref[...] = jnp.full(mx_ref.shape, NEG, F32)

    def pass_a(kt, carry):
        for g in range(2):
            sc = scores(g, kt, None)
            s_ref[g, kt] = sc
            mx_ref[g] = jnp.maximum(mx_ref[g], _fold_max(sc))
        return carry

    _for_tiles(qi, lambda kt: pass_a(kt, 0))
    for g in range(2):
        sc = scores(g, qi, causal)
        s_ref[g, qi] = sc
        _finish_max(mx_ref, g, sc)

    def v_tile(g, kt):
        k0 = pl.multiple_of(kt * tk, tk)
        return kv_ref[0, pl.ds(k0, tk), 256 + g * 128:384 + g * 128]

    blocks = _softmax_pv(s_ref, mx_ref, acc_ref, n_kt, v_tile, lo, tq)
    for j in range(half):
        o_ref[0, :, j * 128:(j + 1) * 128] = blocks[j].astype(BF16)


def _dsa_call(qb, dkv, iq, iw, qf, lt):
    bsz, s, _ = qb.shape
    tq, tk = TQ, TK
    rows = (N_HEADS // 2) * tq
    return pl.pallas_call(
        _dsa_kernel,
        out_shape=jax.ShapeDtypeStruct((bsz, s, 512), BF16),
        grid=(bsz, s // tq),
        in_specs=[pl.BlockSpec((1, tq, 512), lambda b, i: (b, i, 0)),
                  pl.BlockSpec((1, s, 640), lambda b, i: (b, 0, 0)),
                  pl.BlockSpec((1, tq, 512), lambda b, i: (b, i, 0)),
                  pl.BlockSpec((1, tq, 128), lambda b, i: (b, i, 0)),
                  _const_spec(qf.shape), _const_spec(lt.shape)],
        out_specs=pl.BlockSpec((1, tq, 512), lambda b, i: (b, i, 0)),
        scratch_shapes=[pltpu.VMEM((N_HEADS * tq, 128), BF16),
                        pltpu.VMEM((N_HEADS * HEAD_DIM, tq), BF16),
                        pltpu.VMEM((s // tk, tk, tq), I32),
                        pltpu.VMEM((s // tk, tk, tq), I16),
                        pltpu.VMEM((s // tk, tk, tq), I16),
                        pltpu.VMEM((s // tk, tq, tk), F32),
                        pltpu.VMEM((2, s // tk, rows, tk), F32),
                        pltpu.VMEM((2, rows, 128), F32),
                        pltpu.VMEM((2, rows, 128), F32),
                        pltpu.VMEM((16, tq), F32),
                        pltpu.VMEM((s // tk, 8, tq), F32)],
        compiler_params=_params("arbitrary", "arbitrary"),
        name="dsa",
    )(qb, dkv, iq, iw, qf, lt)


_FF_CHUNK = 256


def _post_kernel(x_ref, oa_ref, ob_ref, gate_ref, mod_ref, g2_ref,
                 woa_ref, wob_ref, wout_ref, wg_ref, wu_ref, wd_ref, o_ref, acc_ref):
    d = x_ref.shape[2]
    m = mod_ref[0]
    gate1, shift2, scale2, gate2 = m[2:3], m[3:4], m[4:5], m[5:6]
    ga = _sigmoid(gate_ref[0, :, 0:d].astype(F32))
    gb = _sigmoid(gate_ref[0, :, d:2 * d].astype(F32))
    y = ga * _dot(oa_ref[0], woa_ref[...]) + gb * _dot(ob_ref[0], wob_ref[...])
    x1 = x_ref[0] + gate1 * _dot(y.astype(BF16), wout_ref[...])
    xn = x1 * lax.rsqrt(jnp.mean(x1 * x1, axis=-1, keepdims=True) + RMS_EPS) * g2_ref[...]
    h2 = (xn * (1.0 + scale2) + shift2).astype(BF16)
    acc_ref[...] = jnp.zeros_like(acc_ref)
    d_ff = wg_ref.shape[1]
    for c in range(0, d_ff, _FF_CHUNK):
        gg = _dot(h2, wg_ref[:, c:c + _FF_CHUNK])
        uu = _dot(h2, wu_ref[:, c:c + _FF_CHUNK])
        a = (gg * _sigmoid(gg) * uu).astype(BF16)
        acc_ref[...] += _dot(a, wd_ref[c:c + _FF_CHUNK, :])
    o_ref[0] = x1 + gate2 * acc_ref[...]


def _post_call(x, oa, ob, gates, mod, g2, woa, wob, wout, wg, wu, wd):
    bsz, s, d = x.shape
    tm = TM_PROJ
    weights = (woa, wob, wout, wg, wu, wd)
    w_specs = [pl.BlockSpec(w.shape, lambda b, i: (0, 0), pipeline_mode=pl.Buffered(1)) for w in weights]
    return pl.pallas_call(
        _post_kernel,
        out_shape=jax.ShapeDtypeStruct((bsz, s, d), F32),
        grid=(bsz, s // tm),
        in_specs=[pl.BlockSpec((1, tm, d), lambda b, i: (b, i, 0)),
                  pl.BlockSpec((1, tm, 512), lambda b, i: (b, i, 0)),
                  pl.BlockSpec((1, tm, 512), lambda b, i: (b, i, 0)),
                  pl.BlockSpec((1, tm, 2 * d), lambda b, i: (b, i, 0)),
                  pl.BlockSpec((1, N_ADA, d), lambda b, i: (b, 0, 0)),
                  _const_spec((1, d))] + w_specs,
        out_specs=pl.BlockSpec((1, tm, d), lambda b, i: (b, i, 0)),
        scratch_shapes=[pltpu.VMEM((tm, d), F32)],
        compiler_params=_params("arbitrary", "arbitrary"),
        name="post",
    )(x, oa, ob, gates, mod, g2, *weights)


def _alibi_slopes(n):
    return np.exp2(-8.0 * np.arange(1, n + 1, dtype=np.float64) / n)


def _query_features(low_heads):
    sl = _alibi_slopes(N_HEADS)
    qf = np.zeros((N_HEADS, 128), np.float32)
    for h in range(N_HEADS):
        base = 64 if h in low_heads else 0
        qf[h, base] = 64.0 * sl[h]
        qf[h, base + 1] = sl[h]
    return jnp.asarray(qf, BF16)


def _block_diag_mean(n):
    i = np.arange(n)
    return jnp.asarray((i[:, None] // HEAD_DIM == i[None, :] // HEAD_DIM) / HEAD_DIM, BF16)


def _pad_cols(w, n):
    return jnp.pad(w, ((0, 0), (0, n - w.shape[1])))


def _layer(x, c, w_ada, b_ada, g_norm1, g_norm2, w_in, g_q_a, g_kc_a, g_ks_a, g_kw_a,
           pe_ck, pe_cv, w_ck1, w_ck2, w_cv1, w_cv2, g_q_b, g_k_b, w_o_a, w_o_b, w_out,
           w_ff_gate, w_ff_up, w_ff_down):
    bsz, s, d = x.shape
    hd = HEAD_DIM
    n_slc = s // SLC_LEN

    perm = np.array([(j + 4 * half) * hd + dd for j in range(4) for half in range(2) for dd in range(hd)])

    sizes = (512, 128, 128, 128, 128, 128, 128, 24, 512, 64, 64, 512, 64, 8, d, d)
    offs = np.concatenate([[0], np.cumsum(sizes)])
    col = lambda i: w_in[:, offs[i]:offs[i + 1]]
    dup = lambda w: jnp.concatenate([w, w], axis=1)
    w_all = jnp.concatenate([
        col(0)[:, perm], col(1), col(2), col(3), col(4), col(5), col(6),
        col(8), dup(col(9)), dup(col(10)), dup(col(12)), _pad_cols(col(13), 128),
        col(11) * (hd ** -0.5), _pad_cols(col(7), 128), col(14), col(15)], axis=1).astype(BF16)
    assert w_all.shape[1] == _C_END
    qscale = hd ** -0.5
    gains = jnp.concatenate([jnp.tile(g_q_a, 8) * qscale, jnp.tile(g_ks_a, 2), jnp.tile(g_kw_a, 2),
                             jnp.tile(g_q_b, 8) * qscale, jnp.tile(g_k_b, 2)]).reshape(1, _G_END)
    bd = _block_diag_mean(256)

    mod = _mod_call(c, w_ada, b_ada).reshape(bsz, N_ADA, d)
    qa, kvc, nkv, gn, qb, dkv, iq, iw, gates = _inproj_call(
        x, mod, g_norm1.reshape(1, d), w_all, bd, gains)

    half_len = CMP_LEN // 2

    def cmp_weight(w1k, w1v, first):
        wk = w1k.reshape(CMP_LEN, hd, CMP_HIDDEN)
        wv = w1v.reshape(CMP_LEN, hd, CMP_HIDDEN)
        sl = slice(0, half_len) if first else slice(half_len, CMP_LEN)
        big = jnp.zeros((half_len, 4 * hd, 4 * CMP_HIDDEN), F32)
        for j, wsrc in enumerate((wk, wk, wv, wv)):
            big = big.at[:, j * hd:(j + 1) * hd, j * CMP_HIDDEN:(j + 1) * CMP_HIDDEN].set(wsrc[sl])
        return big.astype(BF16)

    w2 = jnp.zeros((4 * CMP_HIDDEN, 4 * hd), F32)
    for j, wsrc in enumerate((w_ck2, w_ck2, w_cv2, w_cv2)):
        w2 = w2.at[j * CMP_HIDDEN:(j + 1) * CMP_HIDDEN, j * hd:(j + 1) * hd].set(wsrc)
    ckv = _compress_call(
        kvc, cmp_weight(w_ck1, w_cv1, True), cmp_weight(w_ck1, w_cv1, False),
        pe_ck.reshape(1, CMP_LEN * hd), pe_cv.reshape(1, CMP_LEN * hd),
        w_ck1.astype(BF16), w_cv1.astype(BF16), w2.astype(BF16),
        jnp.tile(g_kc_a, 2).reshape(1, 128), _block_diag_mean(128))

    n_cmp_pad = s // CMP_STRIDE
    cs = np.arange(n_cmp_pad)[:, None] * CMP_STRIDE
    jj = np.arange(128)[None, :]
    ov = ((cs < (jj + 1) * SLC_LEN) & (cs + CMP_LEN > jj * SLC_LEN) & (jj < n_slc)
          & (np.arange(n_cmp_pad)[:, None] < (s - CMP_LEN) // CMP_STRIDE + 1))
    o_a = _nsa_call(qa, ckv, nkv, gn, _query_features(range(4)), jnp.asarray(ov.T[:n_slc], BF16))

    lt = jnp.asarray(np.arange(TK)[None, :] < np.arange(TK)[:, None], BF16)
    o_b = _dsa_call(qb, dkv, iq, iw, _query_features(range(0, N_HEADS, 2)), lt)

    return _post_call(x, o_a, o_b, gates, mod, g_norm2.reshape(1, d),
                      w_o_a[perm].astype(BF16), w_o_b.astype(BF16), w_out.astype(BF16),
                      w_ff_gate.astype(BF16), w_ff_up.astype(BF16), w_ff_down.astype(BF16))


def kernel(x, c, w_ada, b_ada, g_norm1, g_norm2, w_in, g_q_a, g_kc_a, g_ks_a, g_kw_a, pe_ck, pe_cv, w_ck1, w_ck2, w_cv1, w_cv2, g_q_b, g_k_b, w_o_a, w_o_b, w_out, w_ff_gate, w_ff_up, w_ff_down):
    for l in range(w_ada.shape[0]):
        x = _layer(x, c, w_ada[l], b_ada[l], g_norm1[l], g_norm2[l], w_in[l],
                   g_q_a[l], g_kc_a[l], g_ks_a[l], g_kw_a[l], pe_ck[l], pe_cv[l],
                   w_ck1[l], w_ck2[l], w_cv1[l], w_cv2[l], g_q_b[l], g_k_b[l],
                   w_o_a[l], w_o_b[l], w_out[l], w_ff_gate[l], w_ff_up[l], w_ff_down[l])
    return x
```

```python
import numpy as np
import jax
import jax.numpy as jnp
from jax import lax
from jax.experimental import pallas as pl
from jax.experimental.pallas import tpu as pltpu

F32 = jnp.float32
BF16 = jnp.bfloat16
I32 = jnp.int32
I16 = jnp.int16

HEAD_DIM = 64
N_HEADS = 8
NSA_GROUPS = 2
CMP_LEN = 32
CMP_STRIDE = 16
CMP_HIDDEN = 128
SLC_LEN = 64
SLC_TOPN = 16
WINDOW = 512
DSA_TOPK_MAX = 256
N_ADA = 6
RMS_EPS = 1e-6
FORCE_SCORE = 1e9
NEG = -1e30
MASK_BIG = 2.0 ** 100
HALF16 = 1 << 15
BLOCK_LANE_HI = 72
BLOCK_LANE_LO = 8

TQ = 256
TK = 256
ATTN_UNROLL = 4
TM_PROJ = 512
VMEM_LIMIT = 56 * 1024 * 1024


def _dot(a, b):
    return jnp.dot(a, b, preferred_element_type=F32)


def _dot_nt(a, b):
    return lax.dot_general(a, b, (((1,), (1,)), ((), ())), preferred_element_type=F32)


def _sigmoid(x):
    return 1.0 / (1.0 + jnp.exp(-x))


def _params(*sem):
    return pltpu.CompilerParams(dimension_semantics=sem, vmem_limit_bytes=VMEM_LIMIT)


def _const_spec(shape):
    nd = len(shape)
    return pl.BlockSpec(shape, lambda *_: (0,) * nd)


def _mod_kernel(c_ref, w_ref, b_ref, o_ref):
    c = c_ref[...]
    a = c * _sigmoid(c)
    o_ref[...] = jnp.dot(a, w_ref[...], preferred_element_type=F32,
                         precision=lax.Precision.HIGHEST) + b_ref[...]


def _mod_call(c, w_ada, b_ada):
    bsz, d = c.shape
    n = w_ada.shape[1]
    tn = 512
    return pl.pallas_call(
        _mod_kernel,
        out_shape=jax.ShapeDtypeStruct((bsz, n), F32),
        grid=(n // tn,),
        in_specs=[pl.BlockSpec((bsz, d), lambda j: (0, 0)),
                  pl.BlockSpec((d, tn), lambda j: (0, j)),
                  pl.BlockSpec((1, tn), lambda j: (0, j))],
        out_specs=pl.BlockSpec((bsz, tn), lambda j: (0, j)),
        compiler_params=_params("arbitrary"),
        name="mod",
    )(c, w_ada, b_ada.reshape(1, n))


_C_QA, _C_KVC, _C_KS, _C_KW, _C_QB, _C_KB, _C_IK, _C_IQ, _C_GN, _C_GATE = (
    0, 512, 768, 1024, 1280, 1792, 2048, 2304, 2816, 2944)
_C_END = 4992
_G_QA, _G_KS, _G_KW, _G_QB, _G_KB, _G_END = 0, 512, 640, 768, 1280, 1408


def _pos_features(pos, lane, base):
    a = (pos >> 6).astype(F32)
    b = (pos & 63).astype(F32)
    return jnp.where(lane == base, a, jnp.where(lane == base + 1, b, 0.0))


def _inproj_kernel(x_ref, mod_ref, g1_ref, w_ref, bd_ref, gain_ref,
                   qa_ref, kvc_ref, nkv_ref, gn_ref, qb_ref, dkv_ref, iq_ref, iw_ref, gate_ref):
    tm = x_ref.shape[1]
    x = x_ref[0]
    m = mod_ref[0]
    shift, scale = m[0:1], m[1:2]
    xn = x * lax.rsqrt(jnp.mean(x * x, axis=-1, keepdims=True) + RMS_EPS) * g1_ref[...]
    h = (xn * (1.0 + scale) + shift).astype(BF16)

    def seg(a, b):
        return _dot(h, w_ref[:, a:b])

    def normed(acc, goff):
        w = acc.shape[1]
        cw = min(w, 256)
        outs = []
        for c in range(0, w, cw):
            a = acc[:, c:c + cw]
            ms = _dot((a * a).astype(BF16), bd_ref[:cw, :cw])
            outs.append(a * lax.rsqrt(ms + RMS_EPS) * gain_ref[:, goff + c:goff + c + cw])
        return outs[0] if len(outs) == 1 else jnp.concatenate(outs, axis=1)

    row = lax.broadcasted_iota(I32, (tm, 128), 0)
    lane = lax.broadcasted_iota(I32, (tm, 128), 1)
    lo = lane < 64
    pos = pl.program_id(1) * tm + row
    feat_hi = _pos_features(pos, lane, 64)
    feat_lo = _pos_features(pos, lane, 0)

    qa_ref[0] = normed(seg(_C_QA, _C_QA + 512), _G_QA).astype(BF16)
    kvc = seg(_C_KVC, _C_KVC + 256)
    kvc_ref[0, 0] = kvc[:, 0:128]
    kvc_ref[0, 1] = kvc[:, 128:256]

    def kv_pair(ref, c0, col, goff, block_mark=False):
        kv = seg(col, col + 256)
        kn = normed(kv[:, 0:128], goff)
        v = kv[:, 128:256]
        f_hi, f_lo = feat_hi, feat_lo
        if block_mark:
            blk = pos >> 6
            f_hi = f_hi + jnp.where(lane == BLOCK_LANE_HI + blk, MASK_BIG, 0.0)
            f_lo = f_lo + jnp.where(lane == BLOCK_LANE_LO + blk, MASK_BIG, 0.0)
        ref[0, :, c0:c0 + 128] = jnp.where(lo, kn, f_hi).astype(BF16)
        ref[0, :, c0 + 128:c0 + 256] = jnp.where(lo, f_lo, kn).astype(BF16)
        ref[0, :, c0 + 256:c0 + 384] = jnp.where(lo, v, 1.0).astype(BF16)
        ref[0, :, c0 + 384:c0 + 512] = jnp.where(lo, 1.0, v).astype(BF16)

    kv_pair(nkv_ref, 0, _C_KS, _G_KS, block_mark=True)
    kv_pair(nkv_ref, 512, _C_KW, _G_KW)
    qb_ref[0] = normed(seg(_C_QB, _C_QB + 512), _G_QB).astype(BF16)
    kv_pair(dkv_ref, 0, _C_KB, _G_KB)

    ikw = seg(_C_IK, _C_IK + 256)
    dkv_ref[0, :, 512:640] = ikw[:, 0:128].astype(BF16)
    iw_ref[0] = ikw[:, 128:256] * (N_HEADS ** -0.5)
    iq_ref[0] = seg(_C_IQ, _C_IQ + 512).astype(BF16)
    gn_ref[0] = seg(_C_GN, _C_GN + 128)
    for c in range(0, 2048, 512):
        gate_ref[0, :, c:c + 512] = seg(_C_GATE + c, _C_GATE + c + 512).astype(BF16)


def _inproj_call(x, mod, g1, w_all, bd, gains):
    bsz, s, d = x.shape
    tm = TM_PROJ
    widths = (512, 256, 1024, 128, 512, 640, 512, 128, 2048)
    dtypes = (BF16, F32, BF16, F32, BF16, BF16, BF16, F32, BF16)
    out_shape = [jax.ShapeDtypeStruct((bsz, s, w), dt) for w, dt in zip(widths, dtypes)]
    out_specs = [pl.BlockSpec((1, tm, w), lambda b, i: (b, i, 0)) for w in widths]
    out_shape[1] = jax.ShapeDtypeStruct((bsz, 2, s, 128), F32)
    out_specs[1] = pl.BlockSpec((1, 2, tm, 128), lambda b, i: (b, 0, i, 0))
    return pl.pallas_call(
        _inproj_kernel,
        out_shape=out_shape,
        grid=(bsz, s // tm),
        in_specs=[pl.BlockSpec((1, tm, d), lambda b, i: (b, i, 0)),
                  pl.BlockSpec((1, N_ADA, d), lambda b, i: (b, 0, 0)),
                  _const_spec((1, d)),
                  _const_spec(w_all.shape),
                  _const_spec(bd.shape),
                  _const_spec(gains.shape)],
        out_specs=out_specs,
        compiler_params=_params("arbitrary", "arbitrary"),
        name="inproj",
    )(x, mod, g1, w_all, bd, gains)


def _compress_kernel(t_ref, wa_ref, wb_ref, pek_ref, pev_ref, w1k_ref, w1v_ref, w2_ref,
                     gkc_ref, bd_ref, o_ref):
    n = t_ref.shape[2] // CMP_STRIDE
    first = jnp.zeros((n, wa_ref.shape[2]), F32)
    second = jnp.zeros((n, wb_ref.shape[2]), F32)
    for l in range(CMP_STRIDE):
        rows_l = pl.ds(l, n, stride=CMP_STRIDE)
        tok = jnp.concatenate([t_ref[0, 0, rows_l, :], t_ref[0, 1, rows_l, :]], axis=1).astype(BF16)
        first = first + _dot(tok, wa_ref[l])
        second = second + _dot(tok, wb_ref[l])
    second = pltpu.roll(second, n - 1, 0)
    ck = _dot(jnp.broadcast_to(pek_ref[...], (8, pek_ref.shape[1])).astype(BF16), w1k_ref[...])[0:1]
    cv = _dot(jnp.broadcast_to(pev_ref[...], (8, pev_ref.shape[1])).astype(BF16), w1v_ref[...])[0:1]
    hid = first + second + jnp.concatenate([ck, ck, cv, cv], axis=1)
    act = hid * _sigmoid(hid)
    out = _dot(act.astype(BF16), w2_ref[...])
    kc = out[:, 0:128]
    ms = _dot((kc * kc).astype(BF16), bd_ref[...])
    kcn = kc * lax.rsqrt(ms + RMS_EPS) * gkc_ref[...]
    row = lax.broadcasted_iota(I32, (n, 128), 0)
    lane = lax.broadcasted_iota(I32, (n, 128), 1)
    lo = lane < 64
    cpos = row * CMP_STRIDE + (CMP_LEN - 1)
    o_ref[0, :, 0:128] = jnp.where(lo, kcn, _pos_features(cpos, lane, 64)).astype(BF16)
    o_ref[0, :, 128:256] = jnp.where(lo, _pos_features(cpos, lane, 0), kcn).astype(BF16)
    o_ref[0, :, 256:384] = out[:, 128:256].astype(BF16)


def _compress_call(kvc, wa, wb, pek, pev, w1k, w1v, w2, gkc, bd):
    bsz, _, s, k = kvc.shape
    n = s // CMP_STRIDE
    args = (wa, wb, pek, pev, w1k, w1v, w2, gkc, bd)
    return pl.pallas_call(
        _compress_kernel,
        out_shape=jax.ShapeDtypeStruct((bsz, n, 384), BF16),
        grid=(bsz,),
        in_specs=[pl.BlockSpec((1, 2, s, k), lambda b: (b, 0, 0, 0))] + [_const_spec(a.shape) for a in args],
        out_specs=pl.BlockSpec((1, n, 384), lambda b: (b, 0, 0)),
        compiler_params=_params("arbitrary"),
        name="compress",
    )(kvc, *args)


def _iota2(shape):
    return lax.broadcasted_iota(I32, shape, 0), lax.broadcasted_iota(I32, shape, 1)


def _for_tiles(n, tile_fn, unroll=2):
    def step(p, carry):
        for u in range(unroll):
            tile_fn(unroll * p + u)
        return carry

    lax.fori_loop(0, n >> (unroll.bit_length() - 1), step, 0)
    done = n - (n & (unroll - 1))
    width = unroll // 2
    while width:
        start = done

        @pl.when((n & width) != 0)
        def _(start=start, width=width):
            for u in range(width):
                tile_fn(start + u)

        done = done + (n & width)
        width //= 2


def _fold_max(s):
    out = s[:, 0:128]
    for c in range(128, s.shape[1], 128):
        out = jnp.maximum(out, s[:, c:c + 128])
    return out


def _finish_max(m_ref, g, last):
    mx = jnp.maximum(m_ref[g], _fold_max(last))
    m_ref[g] = jnp.broadcast_to(jnp.max(mx, axis=-1, keepdims=True), mx.shape)


def _merge_halves(lo, a, b):
    return jnp.where(lo, a, b)


def _softmax_pv(s_ref, m_ref, acc_ref, n_tiles, v_tile, lo, tq, unrolled=False):
    acc_ref[...] = jnp.zeros_like(acc_ref)

    def tile(kt):
        for g in range(2):
            s = s_ref[g, kt]
            m_rep = m_ref[g]
            p = jnp.concatenate([jnp.exp(s[:, c:c + 128] - m_rep) for c in range(0, s.shape[1], 128)], axis=1)
            acc_ref[g] += _dot(p.astype(BF16), v_tile(g, kt))

    if unrolled:
        for kt in range(n_tiles):
            tile(kt)
    else:
        _for_tiles(n_tiles, tile, unroll=ATTN_UNROLL)
    outs = []
    for j in range(acc_ref.shape[1] // tq):
        a0 = acc_ref[0, j * tq:(j + 1) * tq, :]
        a1 = acc_ref[1, j * tq:(j + 1) * tq, :]
        num = _merge_halves(lo, a0, a1)
        den = pltpu.roll(_merge_halves(lo, a1, a0), 64, 1)
        outs.append(num / jnp.maximum(den, 1e-30))
    return outs


def _nsa_kernel(qa_ref, ckv_ref, kv_ref, gn_ref, qf_ref, ovt_ref, o_ref,
                qop_ref, s_ref, mx_ref, acc_ref, og_ref, imp_ref, keep_ref):
    tq, tk = TQ, TK
    ng = NSA_GROUPS
    hg = N_HEADS // ng
    rows = hg * tq
    n_slc = ovt_ref.shape[0]
    qi = pl.program_id(1)
    q0 = qi * tq
    row, lane = _iota2((tq, 128))
    lo = lane < 64
    pos = q0 + row

    qblk = qa_ref[0]
    qf = qf_ref[...]
    for j in range(hg):
        blk = qblk[:, j * 128:(j + 1) * 128]
        qop_ref[j * tq:(j + 1) * tq, :] = jnp.where(lo, blk, qf[j:j + 1, :])
        qop_ref[(hg + j) * tq:(hg + j + 1) * tq, :] = jnp.where(lo, qf[hg + j:hg + j + 1, :], blk)

    gt = _sigmoid(gn_ref[0])

    def add_gated(blocks, br, first=False):
        for j in range(hg):
            c0, c1 = 3 * j + br, 3 * (hg + j) + br
            gate = _merge_halves(lo, jnp.broadcast_to(gt[:, c0:c0 + 1], (tq, 128)),
                                 jnp.broadcast_to(gt[:, c1:c1 + 1], (tq, 128)))
            if first:
                og_ref[j * tq:(j + 1) * tq, :] = gate * blocks[j]
            else:
                og_ref[j * tq:(j + 1) * tq, :] += gate * blocks[j]

    def q_of(g):
        return qop_ref[g * rows:(g + 1) * rows, :]

    r2, c2 = _iota2((tq, tk))
    causal = jnp.where(c2 <= r2, 0.0, NEG)
    older = jnp.where(c2 > r2, 0.0, NEG)

    n_cmp = ckv_ref.shape[1]
    nrow, rcol = _iota2((n_cmp, rows))
    seen = (nrow * CMP_STRIDE + (CMP_LEN - 1)) <= q0 + (rcol & (tq - 1))
    seen_add = jnp.where(seen, 0.0, NEG)
    seen_mul = jnp.where(seen, 1.0, 0.0)
    jrow, tcol = _iota2((8, tq))
    post = q0 + tcol
    cur = post >> 6
    vc_t = ckv_ref[0, :, 256:384].astype(F32).T.astype(BF16)
    cmp_t = []
    for g in range(ng):
        kc = ckv_ref[0, :, g * 128:(g + 1) * 128]
        sm = _dot_nt(kc, q_of(g)) + seen_add
        m = jnp.max(sm, axis=0, keepdims=True)
        e = jnp.exp(sm - m) * seen_mul
        p = e / jnp.maximum(jnp.sum(e, axis=0, keepdims=True), 1e-30)
        cmp_t.append(_dot(vc_t, p.astype(BF16)))

        psum = p[:, 0:tq]
        for r in range(1, hg):
            psum = psum + p[:, r * tq:(r + 1) * tq]
        p_hi = psum.astype(BF16)
        p_lo = (psum - p_hi.astype(F32)).astype(BF16)
        imp = _dot(ovt_ref[...], p_hi) + _dot(ovt_ref[...], p_lo)
        parts = []
        for a in range(n_slc // 8):
            jj = jrow + 8 * a
            forced = (jj == 0) | (jj == cur) | (jj == cur - 1)
            visible = (jj * SLC_LEN) <= post
            parts.append(jnp.where(forced, FORCE_SCORE, jnp.where(visible, imp[8 * a:8 * a + 8], -FORCE_SCORE)))
            imp_ref[8 * a:8 * a + 8, :] = parts[a]
        ranks = [jnp.zeros((8, tq), F32) for _ in parts]
        for i in range(n_slc):
            other = imp_ref[i:i + 1, :]
            for a, mine in enumerate(parts):
                if 8 * a > i:
                    ranks[a] = ranks[a] + jnp.where(other >= mine, 1.0, 0.0)
                elif 8 * a + 7 < i:
                    ranks[a] = ranks[a] + jnp.where(other > mine, 1.0, 0.0)
                else:
                    ranks[a] = (ranks[a] + jnp.where(other > mine, 1.0, 0.0)
                                + jnp.where((other == mine) & (jrow + 8 * a > i), 1.0, 0.0))
        base = BLOCK_LANE_LO if g else BLOCK_LANE_HI
        keep_ref[...] = jnp.zeros_like(keep_ref)
        for a, rk in enumerate(ranks):
            keep_ref[base + 8 * a:base + 8 * a + 8, :] = jnp.where(rk < float(SLC_TOPN), 0.0, -1.0)
        flags = keep_ref[...].T
        for j in range(hg):
            h = g * hg + j
            feat = (qf[h:h + 1, :].astype(F32) + flags).astype(BF16)
            blk = qblk[:, j * 128:(j + 1) * 128]
            qop_ref[h * tq:(h + 1) * tq, :] = jnp.where(lo, feat, blk) if g else jnp.where(lo, blk, feat)
    add_gated([jnp.concatenate([cmp_t[0][0:64, j * tq:(j + 1) * tq],
                                cmp_t[1][64:128, j * tq:(j + 1) * tq]], axis=0).T
               for j in range(hg)], 0, first=True)

    def slc_scores(g, kt, extra):
        k0 = pl.multiple_of(kt * tk, tk)
        kt_ = kv_ref[0, pl.ds(k0, tk), g * 128:(g + 1) * 128]
        sc = _dot_nt(q_of(g), kt_)
        if extra is not None:
            sc = (sc.reshape(hg, tq, tk) + extra[None]).reshape(rows, tk)
        return sc

    mx_ref[...] = jnp.full(mx_ref.shape, NEG, F32)

    def slc_a(kt, carry):
        for g in range(ng):
            sc = slc_scores(g, kt, None)
            s_ref[g, kt] = sc
            mx_ref[g] = jnp.maximum(mx_ref[g], _fold_max(sc))
        return carry

    _for_tiles(qi, lambda kt: slc_a(kt, 0), unroll=ATTN_UNROLL)
    for g in range(ng):
        sc = slc_scores(g, qi, causal)
        s_ref[g, qi] = sc
        _finish_max(mx_ref, g, sc)

    def slc_v(g, kt):
        k0 = pl.multiple_of(kt * tk, tk)
        return kv_ref[0, pl.ds(k0, tk), 256 + g * 128:384 + g * 128]

    add_gated(_softmax_pv(s_ref, mx_ref, acc_ref, qi + 1, slc_v, lo, tq), 1)

    def win_k0(slot):
        return pl.multiple_of(jnp.maximum(qi - 2 + slot, 0) * tk, tk)

    for g in range(ng):
        mx_ref[g] = jnp.full((rows, 128), NEG, F32)
        for slot, base in enumerate((older, None, causal)):
            back = 2 - slot
            kt_ = kv_ref[0, pl.ds(win_k0(slot), tk), 512 + g * 128:640 + g * 128]
            sc = _dot_nt(q_of(g), kt_).reshape(hg, tq, tk)
            if back > 0:
                gone = jnp.where(qi >= back, 0.0, NEG)
                madd = gone if base is None else base + gone
            else:
                madd = base
            sc = (sc + madd).reshape(rows, tk)
            s_ref[g, slot] = sc
            if slot < 2:
                mx_ref[g] = jnp.maximum(mx_ref[g], _fold_max(sc))
            else:
                _finish_max(mx_ref, g, sc)

    def win_v(g, slot):
        return kv_ref[0, pl.ds(win_k0(slot), tk), 768 + g * 128:896 + g * 128]

    add_gated(_softmax_pv(s_ref, mx_ref, acc_ref, 3, win_v, lo, tq, unrolled=True), 2)

    for j in range(hg):
        o_ref[0, :, j * 128:(j + 1) * 128] = og_ref[j * tq:(j + 1) * tq, :].astype(BF16)


def _nsa_call(qa, ckv, nkv, gn, qf, ovt):
    bsz, s, _ = qa.shape
    tq, tk = TQ, TK
    ng = NSA_GROUPS
    rows = (N_HEADS // ng) * tq
    n_cmp_pad = ckv.shape[1]
    assert BLOCK_LANE_LO + ovt.shape[0] <= 64
    return pl.pallas_call(
        _nsa_kernel,
        out_shape=jax.ShapeDtypeStruct((bsz, s, 512), BF16),
        grid=(bsz, s // tq),
        in_specs=[pl.BlockSpec((1, tq, 512), lambda b, i: (b, i, 0)),
                  pl.BlockSpec((1, n_cmp_pad, 384), lambda b, i: (b, 0, 0)),
                  pl.BlockSpec((1, s, 1024), lambda b, i: (b, 0, 0)),
                  pl.BlockSpec((1, tq, 128), lambda b, i: (b, i, 0)),
                  _const_spec(qf.shape), _const_spec(ovt.shape)],
        out_specs=pl.BlockSpec((1, tq, 512), lambda b, i: (b, i, 0)),
        scratch_shapes=[pltpu.VMEM((N_HEADS * tq, 128), BF16),
                        pltpu.VMEM((ng, s // tk, rows, tk), F32),
                        pltpu.VMEM((ng, rows, 128), F32),
                        pltpu.VMEM((ng, rows, 128), F32),
                        pltpu.VMEM((rows, 128), F32),
                        pltpu.VMEM((ovt.shape[0], tq), F32),
                        pltpu.VMEM((128, tq), F32)],
        compiler_params=_params("arbitrary", "arbitrary"),
        name="nsa",
    )(qa, ckv, nkv, gn, qf, ovt)


def _dsa_kernel(qb_ref, kv_ref, iq_ref, iw_ref, qf_ref, lt_ref, o_ref,
                qop_ref, iqt_ref, key_ref, khi_ref, klo_ref, mk_ref, s_ref, mx_ref, acc_ref, cnt_ref, eqc_ref):
    tq, tk = TQ, TK
    half = N_HEADS // 2
    rows = half * tq
    qi = pl.program_id(1)
    n_kt = qi + 1
    _, lane = _iota2((tq, 128))
    lo = lane < 64

    qblk = qb_ref[0]
    qf = qf_ref[...]
    for j in range(half):
        qb_j = qblk[:, j * 128:(j + 1) * 128]
        qop_ref[j * tq:(j + 1) * tq, :] = jnp.where(lo, qb_j, qf[2 * j:2 * j + 1, :])
        qop_ref[(half + j) * tq:(half + j + 1) * tq, :] = jnp.where(lo, qf[2 * j + 1:2 * j + 2, :], qb_j)

    topk = min(DSA_TOPK_MAX, kv_ref.shape[1] // 4)

    @pl.when((qi + 1) * tq > topk)
    def _():
        _dsa_select(qi, float(topk), kv_ref, iq_ref, iw_ref, lt_ref,
                    iqt_ref, key_ref, khi_ref, klo_ref, mk_ref, cnt_ref, eqc_ref)

    @pl.when((qi + 1) * tq <= topk)
    def _():
        def clear(kt):
            mk_ref[kt] = jnp.zeros((tq, tk), F32)
        _for_tiles(n_kt, clear)

    _dsa_attend(qi, lo, kv_ref, qop_ref, mk_ref, s_ref, mx_ref, acc_ref, o_ref)


def _dsa_select(qi, topk, kv_ref, iq_ref, iw_ref, lt_ref, iqt_ref, key_ref, khi_ref, klo_ref, mk_ref,
                cnt_ref, eqc_ref):
    tq, tk = TQ, TK
    n_kt = qi + 1

    iqt_ref[...] = iq_ref[0].astype(F32).T.astype(BF16)
    w_t = iw_ref[0].T[0:N_HEADS]
    kidx, qidx = _iota2((tk, tq))

    def idx_tile(kt):
        k0 = pl.multiple_of(kt * tk, tk)
        ik = kv_ref[0, pl.ds(k0, tk), 512:512 + HEAD_DIM]
        tot = jnp.zeros((tk, tq), F32)
        for h in range(N_HEADS):
            sc = _dot(ik, iqt_ref[h * HEAD_DIM:(h + 1) * HEAD_DIM, :])
            tot = tot + w_t[h:h + 1, :] * jnp.maximum(sc, 0.0)
        past = jnp.where(kt < qi, tk, 0)
        tot = jnp.where(kidx <= qidx + past, tot, -jnp.inf)
        bits = pltpu.bitcast(tot, I32)
        key = bits ^ ((bits >> 31) & 0x7FFFFFFF)
        key_ref[kt] = key
        khi_ref[kt] = (key >> 16).astype(I16)
        klo_ref[kt] = ((key & 0xFFFF) - HALF16).astype(I16)

    _for_tiles(n_kt, idx_tile)


    def count16(ref, pred_fn):
        def tile(kt, c):
            hit = jnp.where(pred_fn(ref[kt]), jnp.bfloat16(1), jnp.bfloat16(0))
            parts = [hit[r * 16:(r + 1) * 16] for r in range(tk // 16)]
            while len(parts) > 1:
                parts = [parts[i] + parts[i + 1] for i in range(0, len(parts), 2)]
            return c + parts[0].astype(F32)

        c = lax.fori_loop(0, n_kt, tile, jnp.zeros((16, tq), F32))
        return jnp.sum(c, axis=0, keepdims=True)

    def kth_half(ref, kth):
        def bit_body(i, prefix):
            bm = jnp.left_shift(jnp.int32(1), 15 - i)
            cand = ((prefix | bm) - HALF16).astype(I16)
            tot = count16(ref, lambda h: h >= cand)
            return jnp.where(tot >= kth, prefix | bm, prefix)
        return lax.fori_loop(0, 16, bit_body, jnp.zeros((1, tq), I32))

    hi_u = kth_half(khi_ref, topk)
    hi16 = (hi_u - HALF16).astype(I16)
    left = topk - count16(khi_ref, lambda h: h > hi16)

    def bucket_tile(kt):
        klo_ref[kt] = jnp.where(khi_ref[kt] == hi16, klo_ref[kt], jnp.int16(-HALF16))

    _for_tiles(n_kt, bucket_tile)
    lo_u = kth_half(klo_ref, left)
    thr = ((hi_u - HALF16) << 16) | lo_u

    cnt_ref[...] = jnp.zeros_like(cnt_ref)

    def gt_tile(kt):
        k = key_ref[kt]
        hit = jnp.where(k > thr, 1.0, 0.0)
        cnt_ref[0:8, :] += jnp.sum(hit.reshape(tk // 8, 8, tq), axis=0)
        eqc_ref[kt] = jnp.sum(jnp.where(k == thr, 1.0, 0.0).reshape(tk // 8, 8, tq), axis=0)

    _for_tiles(n_kt, gt_tile)
    need = topk - jnp.sum(cnt_ref[0:8, :], axis=0, keepdims=True)

    def prefix_tile(kt, run):
        tot = jnp.sum(eqc_ref[kt], axis=0, keepdims=True)
        eqc_ref[kt] = jnp.broadcast_to(run, (8, tq))
        return run + tot

    lax.fori_loop(0, n_kt, prefix_tile, jnp.zeros((1, tq), F32))

    def mask_tile(kt):
        k = key_ref[kt]
        eq = k == thr
        before = _dot(lt_ref[...], jnp.where(eq, 1.0, 0.0).astype(BF16)) + eqc_ref[kt][0:1, :]
        tie = jnp.where(before < need, 0.0, NEG)
        m_t = jnp.where(k > thr, 0.0, jnp.where(eq, tie, NEG))
        mk_ref[kt] = m_t.T

    _for_tiles(n_kt, mask_tile)


def _dsa_attend(qi, lo, kv_ref, qop_ref, mk_ref, s_ref, mx_ref, acc_ref, o_ref):
    tq, tk = TQ, TK
    half = N_HEADS // 2
    rows = half * tq
    n_kt = qi + 1
    r2, c2 = _iota2((tq, tk))
    causal = jnp.where(c2 <= r2, 0.0, NEG)

    def scores(g, kt, extra):
        k0 = pl.multiple_of(kt * tk, tk)
        kt_ = kv_ref[0, pl.ds(k0, tk), g * 128:(g + 1) * 128]
        sc = _dot_nt(qop_ref[g * rows:(g + 1) * rows, :], kt_).reshape(half, tq, tk)
        madd = mk_ref[kt]
        if extra is not None:
            madd = madd + extra
        return (sc + madd[None]).reshape(rows, tk)

    mx_ref[...] = jnp.full(mx_ref.shape, NEG, F32)

    def pass_a(kt, carry):
        for g in range(2):
            sc = scores(g, kt, None)
            s_ref[g, kt] = sc
            mx_ref[g] = jnp.maximum(mx_ref[g], _fold_max(sc))
        return carry

    _for_tiles(qi, lambda kt: pass_a(kt, 0), unroll=ATTN_UNROLL)
    for g in range(2):
        sc = scores(g, qi, causal)
        s_ref[g, qi] = sc
        _finish_max(mx_ref, g, sc)

    def v_tile(g, kt):
        k0 = pl.multiple_of(kt * tk, tk)
        return kv_ref[0, pl.ds(k0, tk), 256 + g * 128:384 + g * 128]

    blocks = _softmax_pv(s_ref, mx_ref, acc_ref, n_kt, v_tile, lo, tq)
    for j in range(half):
        o_ref[0, :, j * 128:(j + 1) * 128] = blocks[j].astype(BF16)


def _dsa_call(qb, dkv, iq, iw, qf, lt):
    bsz, s, _ = qb.shape
    tq, tk = TQ, TK
    rows = (N_HEADS // 2) * tq
    return pl.pallas_call(
        _dsa_kernel,
        out_shape=jax.ShapeDtypeStruct((bsz, s, 512), BF16),
        grid=(bsz, s // tq),
        in_specs=[pl.BlockSpec((1, tq, 512), lambda b, i: (b, i, 0)),
                  pl.BlockSpec((1, s, 640), lambda b, i: (b, 0, 0)),
                  pl.BlockSpec((1, tq, 512), lambda b, i: (b, i, 0)),
                  pl.BlockSpec((1, tq, 128), lambda b, i: (b, i, 0)),
                  _const_spec(qf.shape), _const_spec(lt.shape)],
        out_specs=pl.BlockSpec((1, tq, 512), lambda b, i: (b, i, 0)),
        scratch_shapes=[pltpu.VMEM((N_HEADS * tq, 128), BF16),
                        pltpu.VMEM((N_HEADS * HEAD_DIM, tq), BF16),
                        pltpu.VMEM((s // tk, tk, tq), I32),
                        pltpu.VMEM((s // tk, tk, tq), I16),
                        pltpu.VMEM((s // tk, tk, tq), I16),
                        pltpu.VMEM((s // tk, tq, tk), F32),
                        pltpu.VMEM((2, s // tk, rows, tk), F32),
                        pltpu.VMEM((2, rows, 128), F32),
                        pltpu.VMEM((2, rows, 128), F32),
                        pltpu.VMEM((16, tq), F32),
                        pltpu.VMEM((s // tk, 8, tq), F32)],
        compiler_params=_params("arbitrary", "arbitrary"),
        name="dsa",
    )(qb, dkv, iq, iw, qf, lt)


_FF_CHUNK = 256


def _post_kernel(x_ref, oa_ref, ob_ref, gate_ref, mod_ref, g2_ref,
                 woa_ref, wob_ref, wout_ref, wg_ref, wu_ref, wd_ref, o_ref, acc_ref):
    d = x_ref.shape[2]
    m = mod_ref[0]
    gate1, shift2, scale2, gate2 = m[2:3], m[3:4], m[4:5], m[5:6]
    ga = _sigmoid(gate_ref[0, :, 0:d].astype(F32))
    gb = _sigmoid(gate_ref[0, :, d:2 * d].astype(F32))
    y = ga * _dot(oa_ref[0], woa_ref[...]) + gb * _dot(ob_ref[0], wob_ref[...])
    x1 = x_ref[0] + gate1 * _dot(y.astype(BF16), wout_ref[...])
    xn = x1 * lax.rsqrt(jnp.mean(x1 * x1, axis=-1, keepdims=True) + RMS_EPS) * g2_ref[...]
    h2 = (xn * (1.0 + scale2) + shift2).astype(BF16)
    acc_ref[...] = jnp.zeros_like(acc_ref)
    d_ff = wg_ref.shape[1]
    for c in range(0, d_ff, _FF_CHUNK):
        gg = _dot(h2, wg_ref[:, c:c + _FF_CHUNK])
        uu = _dot(h2, wu_ref[:, c:c + _FF_CHUNK])
        a = (gg * _sigmoid(gg) * uu).astype(BF16)
        acc_ref[...] += _dot(a, wd_ref[c:c + _FF_CHUNK, :])
    o_ref[0] = x1 + gate2 * acc_ref[...]


def _post_call(x, oa, ob, gates, mod, g2, woa, wob, wout, wg, wu, wd):
    bsz, s, d = x.shape
    tm = TM_PROJ
    weights = (woa, wob, wout, wg, wu, wd)
    w_specs = [pl.BlockSpec(w.shape, lambda b, i: (0, 0), pipeline_mode=pl.Buffered(1)) for w in weights]
    return pl.pallas_call(
        _post_kernel,
        out_shape=jax.ShapeDtypeStruct((bsz, s, d), F32),
        grid=(bsz, s // tm),
        in_specs=[pl.BlockSpec((1, tm, d), lambda b, i: (b, i, 0)),
                  pl.BlockSpec((1, tm, 512), lambda b, i: (b, i, 0)),
                  pl.BlockSpec((1, tm, 512), lambda b, i: (b, i, 0)),
                  pl.BlockSpec((1, tm, 2 * d), lambda b, i: (b, i, 0)),
                  pl.BlockSpec((1, N_ADA, d), lambda b, i: (b, 0, 0)),
                  _const_spec((1, d))] + w_specs,
        out_specs=pl.BlockSpec((1, tm, d), lambda b, i: (b, i, 0)),
        scratch_shapes=[pltpu.VMEM((tm, d), F32)],
        compiler_params=_params("arbitrary", "arbitrary"),
        name="post",
    )(x, oa, ob, gates, mod, g2, *weights)


def _alibi_slopes(n):
    return np.exp2(-8.0 * np.arange(1, n + 1, dtype=np.float64) / n)


def _query_features(low_heads):
    sl = _alibi_slopes(N_HEADS)
    qf = np.zeros((N_HEADS, 128), np.float32)
    for h in range(N_HEADS):
        base = 64 if h in low_heads else 0
        qf[h, base] = 64.0 * sl[h]
        qf[h, base + 1] = sl[h]
    return jnp.asarray(qf, BF16)


def _block_diag_mean(n):
    i = np.arange(n)
    return jnp.asarray((i[:, None] // HEAD_DIM == i[None, :] // HEAD_DIM) / HEAD_DIM, BF16)


def _pad_cols(w, n):
    return jnp.pad(w, ((0, 0), (0, n - w.shape[1])))


def _layer(x, c, w_ada, b_ada, g_norm1, g_norm2, w_in, g_q_a, g_kc_a, g_ks_a, g_kw_a,
           pe_ck, pe_cv, w_ck1, w_ck2, w_cv1, w_cv2, g_q_b, g_k_b, w_o_a, w_o_b, w_out,
           w_ff_gate, w_ff_up, w_ff_down):
    bsz, s, d = x.shape
    hd = HEAD_DIM
    n_slc = s // SLC_LEN

    hg = N_HEADS // NSA_GROUPS
    w_qa = w_in[:, :N_HEADS * hd].reshape(d, NSA_GROUPS, hg, hd).transpose(0, 2, 1, 3).reshape(d, N_HEADS * hd)
    w_oa = w_o_a.reshape(NSA_GROUPS, hg, hd, d).transpose(1, 0, 2, 3).reshape(N_HEADS * hd, d)

    sizes = (512, 128, 128, 128, 128, 128, 128, 24, 512, 64, 64, 512, 64, 8, d, d)
    offs = np.concatenate([[0], np.cumsum(sizes)])
    assert w_in.shape == (d, offs[-1]) and d == N_HEADS * 2 * hd
    assert TQ == TK and WINDOW == 2 * TK and s % TQ == 0 and s % TM_PROJ == 0 and s % CMP_STRIDE == 0
    col = lambda i: w_in[:, offs[i]:offs[i + 1]]
    dup = lambda w: jnp.concatenate([w, w], axis=1)
    w_all = jnp.concatenate([
        w_qa, col(1), col(2), col(3), col(4), col(5), col(6),
        col(8), dup(col(9)), dup(col(10)), dup(col(12)), _pad_cols(col(13), 128),
        col(11) * (hd ** -0.5), _pad_cols(col(7), 128), col(14), col(15)], axis=1).astype(BF16)
    assert w_all.shape[1] == _C_END
    qscale = hd ** -0.5
    gains = jnp.concatenate([jnp.tile(g_q_a, 8) * qscale, jnp.tile(g_ks_a, 2), jnp.tile(g_kw_a, 2),
                             jnp.tile(g_q_b, 8) * qscale, jnp.tile(g_k_b, 2)]).reshape(1, _G_END)
    bd = _block_diag_mean(256)

    mod = _mod_call(c, w_ada, b_ada).reshape(bsz, N_ADA, d)
    qa, kvc, nkv, gn, qb, dkv, iq, iw, gates = _inproj_call(
        x, mod, g_norm1.reshape(1, d), w_all, bd, gains)

    half_len = CMP_LEN // 2

    def cmp_weight(w1k, w1v, first):
        wk = w1k.reshape(CMP_LEN, hd, CMP_HIDDEN)
        wv = w1v.reshape(CMP_LEN, hd, CMP_HIDDEN)
        sl = slice(0, half_len) if first else slice(half_len, CMP_LEN)
        big = jnp.zeros((half_len, 4 * hd, 4 * CMP_HIDDEN), F32)
        for j, wsrc in enumerate((wk, wk, wv, wv)):
            big = big.at[:, j * hd:(j + 1) * hd, j * CMP_HIDDEN:(j + 1) * CMP_HIDDEN].set(wsrc[sl])
        return big.astype(BF16)

    w2 = jnp.zeros((4 * CMP_HIDDEN, 4 * hd), F32)
    for j, wsrc in enumerate((w_ck2, w_ck2, w_cv2, w_cv2)):
        w2 = w2.at[j * CMP_HIDDEN:(j + 1) * CMP_HIDDEN, j * hd:(j + 1) * hd].set(wsrc)
    ckv = _compress_call(
        kvc, cmp_weight(w_ck1, w_cv1, True), cmp_weight(w_ck1, w_cv1, False),
        pe_ck.reshape(1, CMP_LEN * hd), pe_cv.reshape(1, CMP_LEN * hd),
        w_ck1.astype(BF16), w_cv1.astype(BF16), w2.astype(BF16),
        jnp.tile(g_kc_a, 2).reshape(1, 128), _block_diag_mean(128))

    n_cmp_pad = s // CMP_STRIDE
    cs = np.arange(n_cmp_pad)[:, None] * CMP_STRIDE
    jj = np.arange(128)[None, :]
    ov = ((cs < (jj + 1) * SLC_LEN) & (cs + CMP_LEN > jj * SLC_LEN) & (jj < n_slc)
          & (np.arange(n_cmp_pad)[:, None] < (s - CMP_LEN) // CMP_STRIDE + 1))
    o_a = _nsa_call(qa, ckv, nkv, gn, _query_features(range(4)), jnp.asarray(ov.T[:n_slc], BF16))

    lt = jnp.asarray(np.arange(TK)[None, :] < np.arange(TK)[:, None], BF16)
    o_b = _dsa_call(qb, dkv, iq, iw, _query_features(range(0, N_HEADS, 2)), lt)

    return _post_call(x, o_a, o_b, gates, mod, g_norm2.reshape(1, d),
                      w_oa.astype(BF16), w_o_b.astype(BF16), w_out.astype(BF16),
                      w_ff_gate.astype(BF16), w_ff_up.astype(BF16), w_ff_down.astype(BF16))


def kernel(x, c, w_ada, b_ada, g_norm1, g_norm2, w_in, g_q_a, g_kc_a, g_ks_a, g_kw_a, pe_ck, pe_cv, w_ck1, w_ck2, w_cv1, w_cv2, g_q_b, g_k_b, w_o_a, w_o_b, w_out, w_ff_gate, w_ff_up, w_ff_down):
    for l in range(w_ada.shape[0]):
        x = _layer(x, c, w_ada[l], b_ada[l], g_norm1[l], g_norm2[l], w_in[l],
                   g_q_a[l], g_kc_a[l], g_ks_a[l], g_kw_a[l], pe_ck[l], pe_cv[l],
                   w_ck1[l], w_ck2[l], w_cv1[l], w_cv2[l], g_q_b[l], g_k_b[l],
                   w_o_a[l], w_o_b[l], w_out[l], w_ff_gate[l], w_ff_up[l], w_ff_down[l])
    return x
```

```python
import numpy as np
import jax
import jax.numpy as jnp
from jax import lax
from jax.experimental import pallas as pl
from jax.experimental.pallas import tpu as pltpu

F32 = jnp.float32
BF16 = jnp.bfloat16
I32 = jnp.int32

HEAD_DIM = 64
N_HEADS = 8
NSA_GROUPS = 2
CMP_LEN = 32
CMP_STRIDE = 16
CMP_HIDDEN = 128
SLC_LEN = 64
SLC_TOPN = 16
WINDOW = 512
DSA_TOPK_MAX = 256
N_ADA = 6
RMS_EPS = 1e-6
FORCE_SCORE = 1e9
NEG = -1e30
MASK_BIG = 2.0 ** 100
BLOCK_LANE_HI = 72
BLOCK_LANE_LO = 8

TQ = 256
TK = 256
ATTN_UNROLL = 4
TM_PROJ = 512
VMEM_LIMIT = 56 * 1024 * 1024


def _dot(a, b):
    return jnp.dot(a, b, preferred_element_type=F32)


def _dot_nt(a, b):
    return lax.dot_general(a, b, (((1,), (1,)), ((), ())), preferred_element_type=F32)


def _sigmoid(x):
    return 1.0 / (1.0 + jnp.exp(-x))


def _params(*sem):
    return pltpu.CompilerParams(dimension_semantics=sem, vmem_limit_bytes=VMEM_LIMIT)


def _const_spec(shape):
    nd = len(shape)
    return pl.BlockSpec(shape, lambda *_: (0,) * nd)


def _mod_kernel(c_ref, w_ref, b_ref, o_ref):
    c = c_ref[...]
    a = c * _sigmoid(c)
    o_ref[...] = jnp.dot(a, w_ref[...], preferred_element_type=F32,
                         precision=lax.Precision.HIGHEST) + b_ref[...]


def _mod_call(c, w_ada, b_ada):
    bsz, d = c.shape
    n = w_ada.shape[1]
    tn = 512
    return pl.pallas_call(
        _mod_kernel,
        out_shape=jax.ShapeDtypeStruct((bsz, n), F32),
        grid=(n // tn,),
        in_specs=[pl.BlockSpec((bsz, d), lambda j: (0, 0)),
                  pl.BlockSpec((d, tn), lambda j: (0, j)),
                  pl.BlockSpec((1, tn), lambda j: (0, j))],
        out_specs=pl.BlockSpec((bsz, tn), lambda j: (0, j)),
        compiler_params=_params("arbitrary"),
        name="mod",
    )(c, w_ada, b_ada.reshape(1, n))


_C_QA, _C_KVC, _C_KS, _C_KW, _C_QB, _C_KB, _C_IK, _C_IQ, _C_GN, _C_GATE = (
    0, 512, 768, 1024, 1280, 1792, 2048, 2304, 2816, 2944)
_C_END = 4992
_G_QA, _G_KS, _G_KW, _G_QB, _G_KB, _G_END = 0, 512, 640, 768, 1280, 1408


def _pos_features(pos, lane, base):
    a = (pos >> 6).astype(F32)
    b = (pos & 63).astype(F32)
    return jnp.where(lane == base, a, jnp.where(lane == base + 1, b, 0.0))


def _inproj_kernel(x_ref, mod_ref, g1_ref, w_ref, bd_ref, gain_ref,
                   qa_ref, kvc_ref, nkv_ref, gn_ref, qb_ref, dkv_ref, iq_ref, iw_ref, gate_ref):
    tm = x_ref.shape[1]
    x = x_ref[0]
    m = mod_ref[0]
    shift, scale = m[0:1], m[1:2]
    xn = x * lax.rsqrt(jnp.mean(x * x, axis=-1, keepdims=True) + RMS_EPS) * g1_ref[...]
    h = (xn * (1.0 + scale) + shift).astype(BF16)

    def seg(a, b):
        return _dot(h, w_ref[:, a:b])

    def normed(acc, goff):
        w = acc.shape[1]
        cw = min(w, 256)
        outs = []
        for c in range(0, w, cw):
            a = acc[:, c:c + cw]
            ms = _dot((a * a).astype(BF16), bd_ref[:cw, :cw])
            outs.append(a * lax.rsqrt(ms + RMS_EPS) * gain_ref[:, goff + c:goff + c + cw])
        return outs[0] if len(outs) == 1 else jnp.concatenate(outs, axis=1)

    row = lax.broadcasted_iota(I32, (tm, 128), 0)
    lane = lax.broadcasted_iota(I32, (tm, 128), 1)
    lo = lane < 64
    pos = pl.program_id(1) * tm + row
    feat_hi = _pos_features(pos, lane, 64)
    feat_lo = _pos_features(pos, lane, 0)

    qa_ref[0] = normed(seg(_C_QA, _C_QA + 512), _G_QA).astype(BF16)
    kvc = seg(_C_KVC, _C_KVC + 256)
    kvc_ref[0, 0] = kvc[:, 0:128]
    kvc_ref[0, 1] = kvc[:, 128:256]

    def kv_pair(ref, c0, col, goff, block_mark=False):
        kv = seg(col, col + 256)
        kn = normed(kv[:, 0:128], goff)
        v = kv[:, 128:256]
        f_hi, f_lo = feat_hi, feat_lo
        if block_mark:
            blk = pos >> 6
            f_hi = f_hi + jnp.where(lane == BLOCK_LANE_HI + blk, MASK_BIG, 0.0)
            f_lo = f_lo + jnp.where(lane == BLOCK_LANE_LO + blk, MASK_BIG, 0.0)
        ref[0, :, c0:c0 + 128] = jnp.where(lo, kn, f_hi).astype(BF16)
        ref[0, :, c0 + 128:c0 + 256] = jnp.where(lo, f_lo, kn).astype(BF16)
        ref[0, :, c0 + 256:c0 + 384] = jnp.where(lo, v, 1.0).astype(BF16)
        ref[0, :, c0 + 384:c0 + 512] = jnp.where(lo, 1.0, v).astype(BF16)

    kv_pair(nkv_ref, 0, _C_KS, _G_KS, block_mark=True)
    kv_pair(nkv_ref, 512, _C_KW, _G_KW)
    qb_ref[0] = normed(seg(_C_QB, _C_QB + 512), _G_QB).astype(BF16)
    kv_pair(dkv_ref, 0, _C_KB, _G_KB)

    ikw = seg(_C_IK, _C_IK + 256)
    dkv_ref[0, :, 512:640] = ikw[:, 0:128].astype(BF16)
    iw_ref[0] = ikw[:, 128:256] * (N_HEADS ** -0.5)
    iq_ref[0] = seg(_C_IQ, _C_IQ + 512).astype(BF16)
    gn_ref[0] = seg(_C_GN, _C_GN + 128)
    for c in range(0, 2048, 512):
        gate_ref[0, :, c:c + 512] = seg(_C_GATE + c, _C_GATE + c + 512).astype(BF16)


def _inproj_call(x, mod, g1, w_all, bd, gains):
    bsz, s, d = x.shape
    tm = TM_PROJ
    widths = (512, 256, 1024, 128, 512, 640, 512, 128, 2048)
    dtypes = (BF16, F32, BF16, F32, BF16, BF16, BF16, F32, BF16)
    out_shape = [jax.ShapeDtypeStruct((bsz, s, w), dt) for w, dt in zip(widths, dtypes)]
    out_specs = [pl.BlockSpec((1, tm, w), lambda b, i: (b, i, 0)) for w in widths]
    out_shape[1] = jax.ShapeDtypeStruct((bsz, 2, s, 128), F32)
    out_specs[1] = pl.BlockSpec((1, 2, tm, 128), lambda b, i: (b, 0, i, 0))
    return pl.pallas_call(
        _inproj_kernel,
        out_shape=out_shape,
        grid=(bsz, s // tm),
        in_specs=[pl.BlockSpec((1, tm, d), lambda b, i: (b, i, 0)),
                  pl.BlockSpec((1, N_ADA, d), lambda b, i: (b, 0, 0)),
                  _const_spec((1, d)),
                  _const_spec(w_all.shape),
                  _const_spec(bd.shape),
                  _const_spec(gains.shape)],
        out_specs=out_specs,
        compiler_params=_params("arbitrary", "arbitrary"),
        name="inproj",
    )(x, mod, g1, w_all, bd, gains)


def _compress_kernel(t_ref, wa_ref, wb_ref, pek_ref, pev_ref, w1k_ref, w1v_ref, w2_ref,
                     gkc_ref, bd_ref, o_ref):
    n = t_ref.shape[2] // CMP_STRIDE
    first = jnp.zeros((n, wa_ref.shape[2]), F32)
    second = jnp.zeros((n, wb_ref.shape[2]), F32)
    for l in range(CMP_STRIDE):
        rows_l = pl.ds(l, n, stride=CMP_STRIDE)
        tok = jnp.concatenate([t_ref[0, 0, rows_l, :], t_ref[0, 1, rows_l, :]], axis=1).astype(BF16)
        first = first + _dot(tok, wa_ref[l])
        second = second + _dot(tok, wb_ref[l])
    second = pltpu.roll(second, n - 1, 0)
    ck = _dot(jnp.broadcast_to(pek_ref[...], (8, pek_ref.shape[1])).astype(BF16), w1k_ref[...])[0:1]
    cv = _dot(jnp.broadcast_to(pev_ref[...], (8, pev_ref.shape[1])).astype(BF16), w1v_ref[...])[0:1]
    hid = first + second + jnp.concatenate([ck, ck, cv, cv], axis=1)
    act = hid * _sigmoid(hid)
    out = _dot(act.astype(BF16), w2_ref[...])
    kc = out[:, 0:128]
    ms = _dot((kc * kc).astype(BF16), bd_ref[...])
    kcn = kc * lax.rsqrt(ms + RMS_EPS) * gkc_ref[...]
    row = lax.broadcasted_iota(I32, (n, 128), 0)
    lane = lax.broadcasted_iota(I32, (n, 128), 1)
    lo = lane < 64
    cpos = row * CMP_STRIDE + (CMP_LEN - 1)
    o_ref[0, :, 0:128] = jnp.where(lo, kcn, _pos_features(cpos, lane, 64)).astype(BF16)
    o_ref[0, :, 128:256] = jnp.where(lo, _pos_features(cpos, lane, 0), kcn).astype(BF16)
    o_ref[0, :, 256:384] = out[:, 128:256].astype(BF16)


def _compress_call(kvc, wa, wb, pek, pev, w1k, w1v, w2, gkc, bd):
    bsz, _, s, k = kvc.shape
    n = s // CMP_STRIDE
    args = (wa, wb, pek, pev, w1k, w1v, w2, gkc, bd)
    return pl.pallas_call(
        _compress_kernel,
        out_shape=jax.ShapeDtypeStruct((bsz, n, 384), BF16),
        grid=(bsz,),
        in_specs=[pl.BlockSpec((1, 2, s, k), lambda b: (b, 0, 0, 0))] + [_const_spec(a.shape) for a in args],
        out_specs=pl.BlockSpec((1, n, 384), lambda b: (b, 0, 0)),
        compiler_params=_params("arbitrary"),
        name="compress",
    )(kvc, *args)


def _iota2(shape):
    return lax.broadcasted_iota(I32, shape, 0), lax.broadcasted_iota(I32, shape, 1)


def _for_tiles(n, tile_fn, unroll=2):
    def step(p, carry):
        for u in range(unroll):
            tile_fn(unroll * p + u)
        return carry

    lax.fori_loop(0, n >> (unroll.bit_length() - 1), step, 0)
    done = n - (n & (unroll - 1))
    width = unroll // 2
    while width:
        start = done

        @pl.when((n & width) != 0)
        def _(start=start, width=width):
            for u in range(width):
                tile_fn(start + u)

        done = done + (n & width)
        width //= 2


def _fold_max(s):
    out = s[:, 0:128]
    for c in range(128, s.shape[1], 128):
        out = jnp.maximum(out, s[:, c:c + 128])
    return out


def _finish_max(m_ref, g, last):
    mx = jnp.maximum(m_ref[g], _fold_max(last))
    m_ref[g] = jnp.broadcast_to(jnp.max(mx, axis=-1, keepdims=True), mx.shape)


def _merge_halves(lo, a, b):
    return jnp.where(lo, a, b)


def _softmax_pv(s_ref, m_ref, acc_ref, n_tiles, v_tile, lo, tq, unrolled=False):
    acc_ref[...] = jnp.zeros_like(acc_ref)

    def tile(kt):
        for g in range(2):
            s = s_ref[g, kt]
            m_rep = m_ref[g]
            p = jnp.concatenate([jnp.exp(s[:, c:c + 128] - m_rep) for c in range(0, s.shape[1], 128)], axis=1)
            acc_ref[g] += _dot(p.astype(BF16), v_tile(g, kt))

    if unrolled:
        for kt in range(n_tiles):
            tile(kt)
    else:
        _for_tiles(n_tiles, tile, unroll=ATTN_UNROLL)
    outs = []
    for j in range(acc_ref.shape[1] // tq):
        a0 = acc_ref[0, j * tq:(j + 1) * tq, :]
        a1 = acc_ref[1, j * tq:(j + 1) * tq, :]
        num = _merge_halves(lo, a0, a1)
        den = pltpu.roll(_merge_halves(lo, a1, a0), 64, 1)
        outs.append(num / jnp.maximum(den, 1e-30))
    return outs


def _nsa_kernel(qa_ref, ckv_ref, kv_ref, gn_ref, qf_ref, ovt_ref, o_ref,
                qop_ref, s_ref, mx_ref, acc_ref, og_ref, imp_ref, keep_ref):
    tq, tk = TQ, TK
    ng = NSA_GROUPS
    hg = N_HEADS // ng
    rows = hg * tq
    n_slc = ovt_ref.shape[0]
    qi = pl.program_id(1)
    q0 = qi * tq
    row, lane = _iota2((tq, 128))
    lo = lane < 64
    pos = q0 + row

    qblk = qa_ref[0]
    qf = qf_ref[...]
    for j in range(hg):
        blk = qblk[:, j * 128:(j + 1) * 128]
        qop_ref[j * tq:(j + 1) * tq, :] = jnp.where(lo, blk, qf[j:j + 1, :])
        qop_ref[(hg + j) * tq:(hg + j + 1) * tq, :] = jnp.where(lo, qf[hg + j:hg + j + 1, :], blk)

    gt = _sigmoid(gn_ref[0])

    def add_gated(blocks, br, first=False):
        for j in range(hg):
            c0, c1 = 3 * j + br, 3 * (hg + j) + br
            gate = _merge_halves(lo, jnp.broadcast_to(gt[:, c0:c0 + 1], (tq, 128)),
                                 jnp.broadcast_to(gt[:, c1:c1 + 1], (tq, 128)))
            if first:
                og_ref[j * tq:(j + 1) * tq, :] = gate * blocks[j]
            else:
                og_ref[j * tq:(j + 1) * tq, :] += gate * blocks[j]

    def q_of(g):
        return qop_ref[g * rows:(g + 1) * rows, :]

    r2, c2 = _iota2((tq, tk))
    causal = jnp.where(c2 <= r2, 0.0, NEG)
    older = jnp.where(c2 > r2, 0.0, NEG)

    n_cmp = ckv_ref.shape[1]
    nrow, ncol = _iota2((n_cmp, tq))
    seen = (nrow * CMP_STRIDE + (CMP_LEN - 1)) <= q0 + ncol
    seen_add = jnp.concatenate([jnp.where(seen, 0.0, NEG)] * hg, axis=1)
    seen_mul = jnp.concatenate([jnp.where(seen, 1.0, 0.0)] * hg, axis=1)
    jrow, tcol = _iota2((8, tq))
    post = q0 + tcol
    cur = post >> 6
    vc_t = ckv_ref[0, :, 256:384].astype(F32).T.astype(BF16)
    cmp_t = []
    for g in range(ng):
        kc = ckv_ref[0, :, g * 128:(g + 1) * 128]
        sm = _dot_nt(kc, q_of(g)) + seen_add
        m = jnp.max(sm, axis=0, keepdims=True)
        e = jnp.exp(sm - m) * seen_mul
        p = e / jnp.maximum(jnp.sum(e, axis=0, keepdims=True), 1e-30)
        cmp_t.append(_dot(vc_t, p.astype(BF16)))

        psum = p[:, 0:tq]
        for r in range(1, hg):
            psum = psum + p[:, r * tq:(r + 1) * tq]
        p_hi = psum.astype(BF16)
        p_lo = (psum - p_hi.astype(F32)).astype(BF16)
        imp = _dot(ovt_ref[...], p_hi) + _dot(ovt_ref[...], p_lo)
        parts = []
        for a in range(n_slc // 8):
            jj = jrow + 8 * a
            forced = (jj == 0) | (jj == cur) | (jj == cur - 1)
            visible = (jj * SLC_LEN) <= post
            parts.append(jnp.where(forced, FORCE_SCORE, jnp.where(visible, imp[8 * a:8 * a + 8], -FORCE_SCORE)))
            imp_ref[8 * a:8 * a + 8, :] = parts[a]
        ranks = [jnp.zeros((8, tq), F32) for _ in parts]
        for i in range(n_slc):
            other = imp_ref[i:i + 1, :]
            for a, mine in enumerate(parts):
                if 8 * a > i:
                    ranks[a] = ranks[a] + jnp.where(other >= mine, 1.0, 0.0)
                elif 8 * a + 7 < i:
                    ranks[a] = ranks[a] + jnp.where(other > mine, 1.0, 0.0)
                else:
                    ranks[a] = (ranks[a] + jnp.where(other > mine, 1.0, 0.0)
                                + jnp.where((other == mine) & (jrow + 8 * a > i), 1.0, 0.0))
        base = BLOCK_LANE_LO if g else BLOCK_LANE_HI
        keep_ref[...] = jnp.zeros_like(keep_ref)
        for a, rk in enumerate(ranks):
            keep_ref[base + 8 * a:base + 8 * a + 8, :] = jnp.where(rk < float(SLC_TOPN), 0.0, -1.0)
        flags = keep_ref[...].T
        for j in range(hg):
            h = g * hg + j
            feat = (qf[h:h + 1, :].astype(F32) + flags).astype(BF16)
            blk = qblk[:, j * 128:(j + 1) * 128]
            qop_ref[h * tq:(h + 1) * tq, :] = jnp.where(lo, feat, blk) if g else jnp.where(lo, blk, feat)
    add_gated([jnp.concatenate([cmp_t[0][0:64, j * tq:(j + 1) * tq],
                                cmp_t[1][64:128, j * tq:(j + 1) * tq]], axis=0).T
               for j in range(hg)], 0, first=True)

    def slc_scores(g, kt, extra):
        k0 = pl.multiple_of(kt * tk, tk)
        kt_ = kv_ref[0, pl.ds(k0, tk), g * 128:(g + 1) * 128]
        sc = _dot_nt(q_of(g), kt_)
        if extra is not None:
            sc = (sc.reshape(hg, tq, tk) + extra[None]).reshape(rows, tk)
        return sc

    mx_ref[...] = jnp.full(mx_ref.shape, NEG, F32)

    def slc_a(kt, carry):
        for g in range(ng):
            sc = slc_scores(g, kt, None)
            s_ref[g, kt] = sc
            mx_ref[g] = jnp.maximum(mx_ref[g], _fold_max(sc))
        return carry

    _for_tiles(qi, lambda kt: slc_a(kt, 0), unroll=ATTN_UNROLL)
    for g in range(ng):
        sc = slc_scores(g, qi, causal)
        s_ref[g, qi] = sc
        _finish_max(mx_ref, g, sc)

    def slc_v(g, kt):
        k0 = pl.multiple_of(kt * tk, tk)
        return kv_ref[0, pl.ds(k0, tk), 256 + g * 128:384 + g * 128]

    add_gated(_softmax_pv(s_ref, mx_ref, acc_ref, qi + 1, slc_v, lo, tq), 1)

    def win_k0(slot):
        return pl.multiple_of(jnp.maximum(qi - 2 + slot, 0) * tk, tk)

    for g in range(ng):
        mx_ref[g] = jnp.full((rows, 128), NEG, F32)
        for slot, base in enumerate((older, None, causal)):
            back = 2 - slot
            kt_ = kv_ref[0, pl.ds(win_k0(slot), tk), 512 + g * 128:640 + g * 128]
            sc = _dot_nt(q_of(g), kt_).reshape(hg, tq, tk)
            if back > 0:
                gone = jnp.where(qi >= back, 0.0, NEG)
                madd = gone if base is None else base + gone
            else:
                madd = base
            sc = (sc + madd).reshape(rows, tk)
            s_ref[g, slot] = sc
            if slot < 2:
                mx_ref[g] = jnp.maximum(mx_ref[g], _fold_max(sc))
            else:
                _finish_max(mx_ref, g, sc)

    def win_v(g, slot):
        return kv_ref[0, pl.ds(win_k0(slot), tk), 768 + g * 128:896 + g * 128]

    add_gated(_softmax_pv(s_ref, mx_ref, acc_ref, 3, win_v, lo, tq, unrolled=True), 2)

    for j in range(hg):
        o_ref[0, :, j * 128:(j + 1) * 128] = og_ref[j * tq:(j + 1) * tq, :].astype(BF16)


def _nsa_call(qa, ckv, nkv, gn, qf, ovt):
    bsz, s, _ = qa.shape
    tq, tk = TQ, TK
    ng = NSA_GROUPS
    rows = (N_HEADS // ng) * tq
    n_cmp_pad = ckv.shape[1]
    assert BLOCK_LANE_LO + ovt.shape[0] <= 64
    return pl.pallas_call(
        _nsa_kernel,
        out_shape=jax.ShapeDtypeStruct((bsz, s, 512), BF16),
        grid=(bsz, s // tq),
        in_specs=[pl.BlockSpec((1, tq, 512), lambda b, i: (b, i, 0)),
                  pl.BlockSpec((1, n_cmp_pad, 384), lambda b, i: (b, 0, 0)),
                  pl.BlockSpec((1, s, 1024), lambda b, i: (b, 0, 0)),
                  pl.BlockSpec((1, tq, 128), lambda b, i: (b, i, 0)),
                  _const_spec(qf.shape), _const_spec(ovt.shape)],
        out_specs=pl.BlockSpec((1, tq, 512), lambda b, i: (b, i, 0)),
        scratch_shapes=[pltpu.VMEM((N_HEADS * tq, 128), BF16),
                        pltpu.VMEM((ng, s // tk, rows, tk), F32),
                        pltpu.VMEM((ng, rows, 128), F32),
                        pltpu.VMEM((ng, rows, 128), F32),
                        pltpu.VMEM((rows, 128), F32),
                        pltpu.VMEM((ovt.shape[0], tq), F32),
                        pltpu.VMEM((128, tq), F32)],
        compiler_params=_params("arbitrary", "arbitrary"),
        name="nsa",
    )(qa, ckv, nkv, gn, qf, ovt)


def _dsa_kernel(qb_ref, kv_ref, iq_ref, iw_ref, qf_ref, lt_ref, o_ref,
                qop_ref, iqt_ref, key_ref, mk_ref, s_ref, mx_ref, acc_ref, cnt_ref, eqc_ref):
    tq, tk = TQ, TK
    half = N_HEADS // 2
    rows = half * tq
    qi = pl.program_id(1)
    n_kt = qi + 1
    _, lane = _iota2((tq, 128))
    lo = lane < 64

    qblk = qb_ref[0]
    qf = qf_ref[...]
    for j in range(half):
        qb_j = qblk[:, j * 128:(j + 1) * 128]
        qop_ref[j * tq:(j + 1) * tq, :] = jnp.where(lo, qb_j, qf[2 * j:2 * j + 1, :])
        qop_ref[(half + j) * tq:(half + j + 1) * tq, :] = jnp.where(lo, qf[2 * j + 1:2 * j + 2, :], qb_j)

    topk = min(DSA_TOPK_MAX, kv_ref.shape[1] // 4)

    @pl.when((qi + 1) * tq > topk)
    def _():
        _dsa_select(qi, float(topk), kv_ref, iq_ref, iw_ref, lt_ref,
                    iqt_ref, key_ref, mk_ref, cnt_ref, eqc_ref)

    @pl.when((qi + 1) * tq <= topk)
    def _():
        def clear(kt):
            mk_ref[kt] = jnp.zeros((tq, tk), F32)
        _for_tiles(n_kt, clear)

    _dsa_attend(qi, lo, kv_ref, qop_ref, mk_ref, s_ref, mx_ref, acc_ref, o_ref)


def _dsa_select(qi, topk, kv_ref, iq_ref, iw_ref, lt_ref, iqt_ref, key_ref, mk_ref, cnt_ref, eqc_ref):
    tq, tk = TQ, TK
    n_kt = qi + 1

    iqt_ref[...] = iq_ref[0].astype(F32).T.astype(BF16)
    w_t = iw_ref[0].T[0:N_HEADS]
    kidx, qidx = _iota2((tk, tq))

    def idx_tile(kt):
        k0 = pl.multiple_of(kt * tk, tk)
        ik = kv_ref[0, pl.ds(k0, tk), 512:512 + HEAD_DIM]
        tot = jnp.zeros((tk, tq), F32)
        for h in range(N_HEADS):
            sc = _dot(ik, iqt_ref[h * HEAD_DIM:(h + 1) * HEAD_DIM, :])
            tot = tot + w_t[h:h + 1, :] * jnp.maximum(sc, 0.0)
        past = jnp.where(kt < qi, tk, 0)
        key_ref[kt] = jnp.where(kidx <= qidx + past, tot, -jnp.inf)

    _for_tiles(n_kt, idx_tile, unroll=ATTN_UNROLL)

    def count(pred_fn):
        def tile(kt, c):
            hit = jnp.where(pred_fn(key_ref[kt]), 1.0, 0.0)
            return c + jnp.sum(hit.reshape(tk // 8, 8, tq), axis=0)

        c = lax.fori_loop(0, n_kt, tile, jnp.zeros((8, tq), F32))
        return jnp.sum(c, axis=0, keepdims=True)

    def value_at(place):
        key = place ^ (-(2 ** 31))
        return pltpu.bitcast(key ^ ((key >> 31) & 0x7FFFFFFF), F32)

    def bit_body(i, place):
        bm = jnp.left_shift(jnp.int32(1), 31 - i)
        cand = value_at(place | bm)
        tot = count(lambda sc: sc >= cand)
        return jnp.where(tot >= topk, place | bm, place)

    thr = value_at(lax.fori_loop(0, 32, bit_body, jnp.zeros((1, tq), I32)))

    cnt_ref[...] = jnp.zeros_like(cnt_ref)

    def gt_tile(kt):
        k = key_ref[kt]
        hit = jnp.where(k > thr, 1.0, 0.0)
        cnt_ref[0:8, :] += jnp.sum(hit.reshape(tk // 8, 8, tq), axis=0)
        eqc_ref[kt] = jnp.sum(jnp.where(k == thr, 1.0, 0.0).reshape(tk // 8, 8, tq), axis=0)

    _for_tiles(n_kt, gt_tile, unroll=ATTN_UNROLL)
    need = topk - jnp.sum(cnt_ref[0:8, :], axis=0, keepdims=True)

    def prefix_tile(kt, run):
        tot = jnp.sum(eqc_ref[kt], axis=0, keepdims=True)
        eqc_ref[kt] = jnp.broadcast_to(run, (8, tq))
        return run + tot

    lax.fori_loop(0, n_kt, prefix_tile, jnp.zeros((1, tq), F32))

    def mask_tile(kt):
        k = key_ref[kt]
        eq = k == thr
        before = _dot(lt_ref[...], jnp.where(eq, 1.0, 0.0).astype(BF16)) + eqc_ref[kt][0:1, :]
        tie = jnp.where(before < need, 0.0, NEG)
        m_t = jnp.where(k > thr, 0.0, jnp.where(eq, tie, NEG))
        mk_ref[kt] = m_t.T

    _for_tiles(n_kt, mask_tile, unroll=ATTN_UNROLL)


def _dsa_attend(qi, lo, kv_ref, qop_ref, mk_ref, s_ref, mx_ref, acc_ref, o_ref):
    tq, tk = TQ, TK
    half = N_HEADS // 2
    rows = half * tq
    n_kt = qi + 1
    r2, c2 = _iota2((tq, tk))
    causal = jnp.where(c2 <= r2, 0.0, NEG)

    def scores(g, kt, extra):
        k0 = pl.multiple_of(kt * tk, tk)
        kt_ = kv_ref[0, pl.ds(k0, tk), g * 128:(g + 1) * 128]
        sc = _dot_nt(qop_ref[g * rows:(g + 1) * rows, :], kt_).reshape(half, tq, tk)
        madd = mk_ref[kt]
        if extra is not None:
            madd = madd + extra
        return (sc + madd[None]).reshape(rows, tk)

    mx_ref[...] = jnp.full(mx_ref.shape, NEG, F32)

    def pass_a(kt, carry):
        for g in range(2):
            sc = scores(g, kt, None)
            s_ref[g, kt] = sc
            mx_ref[g] = jnp.maximum(mx_ref[g], _fold_max(sc))
        return carry

    _for_tiles(qi, lambda kt: pass_a(kt, 0), unroll=ATTN_UNROLL)
    for g in range(2):
        sc = scores(g, qi, causal)
        s_ref[g, qi] = sc
        _finish_max(mx_ref, g, sc)

    def v_tile(g, kt):
        k0 = pl.multiple_of(kt * tk, tk)
        return kv_ref[0, pl.ds(k0, tk), 256 + g * 128:384 + g * 128]

    blocks = _softmax_pv(s_ref, mx_ref, acc_ref, n_kt, v_tile, lo, tq)
    for j in range(half):
        o_ref[0, :, j * 128:(j + 1) * 128] = blocks[j].astype(BF16)


def _dsa_call(qb, dkv, iq, iw, qf, lt):
    bsz, s, _ = qb.shape
    tq, tk = TQ, TK
    rows = (N_HEADS // 2) * tq
    return pl.pallas_call(
        _dsa_kernel,
        out_shape=jax.ShapeDtypeStruct((bsz, s, 512), BF16),
        grid=(bsz, s // tq),
        in_specs=[pl.BlockSpec((1, tq, 512), lambda b, i: (b, i, 0)),
                  pl.BlockSpec((1, s, 640), lambda b, i: (b, 0, 0)),
                  pl.BlockSpec((1, tq, 512), lambda b, i: (b, i, 0)),
                  pl.BlockSpec((1, tq, 128), lambda b, i: (b, i, 0)),
                  _const_spec(qf.shape), _const_spec(lt.shape)],
        out_specs=pl.BlockSpec((1, tq, 512), lambda b, i: (b, i, 0)),
        scratch_shapes=[pltpu.VMEM((N_HEADS * tq, 128), BF16),
                        pltpu.VMEM((N_HEADS * HEAD_DIM, tq), BF16),
                        pltpu.VMEM((s // tk, tk, tq), F32),
                        pltpu.VMEM((s // tk, tq, tk), F32),
                        pltpu.VMEM((2, s // tk, rows, tk), F32),
                        pltpu.VMEM((2, rows, 128), F32),
                        pltpu.VMEM((2, rows, 128), F32),
                        pltpu.VMEM((16, tq), F32),
                        pltpu.VMEM((s // tk, 8, tq), F32)],
        compiler_params=_params("arbitrary", "arbitrary"),
        name="dsa",
    )(qb, dkv, iq, iw, qf, lt)


_FF_CHUNK = 256


def _post_kernel(x_ref, oa_ref, ob_ref, gate_ref, mod_ref, g2_ref,
                 woa_ref, wob_ref, wout_ref, wg_ref, wu_ref, wd_ref, o_ref, acc_ref):
    d = x_ref.shape[2]
    m = mod_ref[0]
    gate1, shift2, scale2, gate2 = m[2:3], m[3:4], m[4:5], m[5:6]
    ga = _sigmoid(gate_ref[0, :, 0:d].astype(F32))
    gb = _sigmoid(gate_ref[0, :, d:2 * d].astype(F32))
    y = ga * _dot(oa_ref[0], woa_ref[...]) + gb * _dot(ob_ref[0], wob_ref[...])
    x1 = x_ref[0] + gate1 * _dot(y.astype(BF16), wout_ref[...])
    xn = x1 * lax.rsqrt(jnp.mean(x1 * x1, axis=-1, keepdims=True) + RMS_EPS) * g2_ref[...]
    h2 = (xn * (1.0 + scale2) + shift2).astype(BF16)
    acc_ref[...] = jnp.zeros_like(acc_ref)
    d_ff = wg_ref.shape[1]
    for c in range(0, d_ff, _FF_CHUNK):
        gg = _dot(h2, wg_ref[:, c:c + _FF_CHUNK])
        uu = _dot(h2, wu_ref[:, c:c + _FF_CHUNK])
        a = (gg * _sigmoid(gg) * uu).astype(BF16)
        acc_ref[...] += _dot(a, wd_ref[c:c + _FF_CHUNK, :])
    o_ref[0] = x1 + gate2 * acc_ref[...]


def _post_call(x, oa, ob, gates, mod, g2, woa, wob, wout, wg, wu, wd):
    bsz, s, d = x.shape
    tm = TM_PROJ
    weights = (woa, wob, wout, wg, wu, wd)
    w_specs = [pl.BlockSpec(w.shape, lambda b, i: (0, 0), pipeline_mode=pl.Buffered(1)) for w in weights]
    return pl.pallas_call(
        _post_kernel,
        out_shape=jax.ShapeDtypeStruct((bsz, s, d), F32),
        grid=(bsz, s // tm),
        in_specs=[pl.BlockSpec((1, tm, d), lambda b, i: (b, i, 0)),
                  pl.BlockSpec((1, tm, 512), lambda b, i: (b, i, 0)),
                  pl.BlockSpec((1, tm, 512), lambda b, i: (b, i, 0)),
                  pl.BlockSpec((1, tm, 2 * d), lambda b, i: (b, i, 0)),
                  pl.BlockSpec((1, N_ADA, d), lambda b, i: (b, 0, 0)),
                  _const_spec((1, d))] + w_specs,
        out_specs=pl.BlockSpec((1, tm, d), lambda b, i: (b, i, 0)),
        scratch_shapes=[pltpu.VMEM((tm, d), F32)],
        compiler_params=_params("arbitrary", "arbitrary"),
        name="post",
    )(x, oa, ob, gates, mod, g2, *weights)


def _alibi_slopes(n):
    return np.exp2(-8.0 * np.arange(1, n + 1, dtype=np.float64) / n)


def _query_features(low_heads):
    sl = _alibi_slopes(N_HEADS)
    qf = np.zeros((N_HEADS, 128), np.float32)
    for h in range(N_HEADS):
        base = 64 if h in low_heads else 0
        qf[h, base] = 64.0 * sl[h]
        qf[h, base + 1] = sl[h]
    return jnp.asarray(qf, BF16)


def _block_diag_mean(n):
    i = np.arange(n)
    return jnp.asarray((i[:, None] // HEAD_DIM == i[None, :] // HEAD_DIM) / HEAD_DIM, BF16)


def _pad_cols(w, n):
    return jnp.pad(w, ((0, 0), (0, n - w.shape[1])))


def _layer(x, c, w_ada, b_ada, g_norm1, g_norm2, w_in, g_q_a, g_kc_a, g_ks_a, g_kw_a,
           pe_ck, pe_cv, w_ck1, w_ck2, w_cv1, w_cv2, g_q_b, g_k_b, w_o_a, w_o_b, w_out,
           w_ff_gate, w_ff_up, w_ff_down):
    bsz, s, d = x.shape
    hd = HEAD_DIM
    n_slc = s // SLC_LEN

    hg = N_HEADS // NSA_GROUPS
    w_qa = w_in[:, :N_HEADS * hd].reshape(d, NSA_GROUPS, hg, hd).transpose(0, 2, 1, 3).reshape(d, N_HEADS * hd)
    w_oa = w_o_a.reshape(NSA_GROUPS, hg, hd, d).transpose(1, 0, 2, 3).reshape(N_HEADS * hd, d)

    sizes = (512, 128, 128, 128, 128, 128, 128, 24, 512, 64, 64, 512, 64, 8, d, d)
    offs = np.concatenate([[0], np.cumsum(sizes)])
    assert w_in.shape == (d, offs[-1]) and d == N_HEADS * 2 * hd
    assert TQ == TK and WINDOW == 2 * TK and s % TQ == 0 and s % TM_PROJ == 0 and s % CMP_STRIDE == 0
    col = lambda i: w_in[:, offs[i]:offs[i + 1]]
    dup = lambda w: jnp.concatenate([w, w], axis=1)
    w_all = jnp.concatenate([
        w_qa, col(1), col(2), col(3), col(4), col(5), col(6),
        col(8), dup(col(9)), dup(col(10)), dup(col(12)), _pad_cols(col(13), 128),
        col(11) * (hd ** -0.5), _pad_cols(col(7), 128), col(14), col(15)], axis=1).astype(BF16)
    assert w_all.shape[1] == _C_END
    qscale = hd ** -0.5
    gains = jnp.concatenate([jnp.tile(g_q_a, 8) * qscale, jnp.tile(g_ks_a, 2), jnp.tile(g_kw_a, 2),
                             jnp.tile(g_q_b, 8) * qscale, jnp.tile(g_k_b, 2)]).reshape(1, _G_END)
    bd = _block_diag_mean(256)

    mod = _mod_call(c, w_ada, b_ada).reshape(bsz, N_ADA, d)
    qa, kvc, nkv, gn, qb, dkv, iq, iw, gates = _inproj_call(
        x, mod, g_norm1.reshape(1, d), w_all, bd, gains)

    half_len = CMP_LEN // 2

    def cmp_weight(w1k, w1v, first):
        wk = w1k.reshape(CMP_LEN, hd, CMP_HIDDEN)
        wv = w1v.reshape(CMP_LEN, hd, CMP_HIDDEN)
        sl = slice(0, half_len) if first else slice(half_len, CMP_LEN)
        big = jnp.zeros((half_len, 4 * hd, 4 * CMP_HIDDEN), F32)
        for j, wsrc in enumerate((wk, wk, wv, wv)):
            big = big.at[:, j * hd:(j + 1) * hd, j * CMP_HIDDEN:(j + 1) * CMP_HIDDEN].set(wsrc[sl])
        return big.astype(BF16)

    w2 = jnp.zeros((4 * CMP_HIDDEN, 4 * hd), F32)
    for j, wsrc in enumerate((w_ck2, w_ck2, w_cv2, w_cv2)):
        w2 = w2.at[j * CMP_HIDDEN:(j + 1) * CMP_HIDDEN, j * hd:(j + 1) * hd].set(wsrc)
    ckv = _compress_call(
        kvc, cmp_weight(w_ck1, w_cv1, True), cmp_weight(w_ck1, w_cv1, False),
        pe_ck.reshape(1, CMP_LEN * hd), pe_cv.reshape(1, CMP_LEN * hd),
        w_ck1.astype(BF16), w_cv1.astype(BF16), w2.astype(BF16),
        jnp.tile(g_kc_a, 2).reshape(1, 128), _block_diag_mean(128))

    n_cmp_pad = s // CMP_STRIDE
    cs = np.arange(n_cmp_pad)[:, None] * CMP_STRIDE
    jj = np.arange(128)[None, :]
    ov = ((cs < (jj + 1) * SLC_LEN) & (cs + CMP_LEN > jj * SLC_LEN) & (jj < n_slc)
          & (np.arange(n_cmp_pad)[:, None] < (s - CMP_LEN) // CMP_STRIDE + 1))
    o_a = _nsa_call(qa, ckv, nkv, gn, _query_features(range(4)), jnp.asarray(ov.T[:n_slc], BF16))

    lt = jnp.asarray(np.arange(TK)[None, :] < np.arange(TK)[:, None], BF16)
    o_b = _dsa_call(qb, dkv, iq, iw, _query_features(range(0, N_HEADS, 2)), lt)

    return _post_call(x, o_a, o_b, gates, mod, g_norm2.reshape(1, d),
                      w_oa.astype(BF16), w_o_b.astype(BF16), w_out.astype(BF16),
                      w_ff_gate.astype(BF16), w_ff_up.astype(BF16), w_ff_down.astype(BF16))


def kernel(x, c, w_ada, b_ada, g_norm1, g_norm2, w_in, g_q_a, g_kc_a, g_ks_a, g_kw_a, pe_ck, pe_cv, w_ck1, w_ck2, w_cv1, w_cv2, g_q_b, g_k_b, w_o_a, w_o_b, w_out, w_ff_gate, w_ff_up, w_ff_down):
    for l in range(w_ada.shape[0]):
        x = _layer(x, c, w_ada[l], b_ada[l], g_norm1[l], g_norm2[l], w_in[l],
                   g_q_a[l], g_kc_a[l], g_ks_a[l], g_kw_a[l], pe_ck[l], pe_cv[l],
                   w_ck1[l], w_ck2[l], w_cv1[l], w_cv2[l], g_q_b[l], g_k_b[l],
                   w_o_a[l], w_o_b[l], w_out[l], w_ff_gate[l], w_ff_up[l], w_ff_down[l])
    return x
```
